```python
import jax, jax.numpy as jnp
from jax import lax
import numpy as np

D_MODEL = 4096
BATCH = 4
SEQ = 2048
DEPTH = 2
DEC_BATCH = 128
DEC_SEQ = 1
PAST_LEN = 16384
PAGE_SIZE = 128

N_MLA_LAYERS = (DEPTH + 1) // 2
N_RWKV_LAYERS = DEPTH // 2
MIX_WIDTH = D_MODEL
MLA_HEADS = 16
QK_NOPE_DIM = 128
QK_ROPE_DIM = 64
V_HEAD_DIM = 128
Q_LORA_RANK = 1024
KV_LORA_RANK = 512
ROPE_THETA = 10000.0
MLA_SCALE = float((QK_NOPE_DIM + QK_ROPE_DIM) ** -0.5)
Q_BLOCK = 128
CONV_DIM = MIX_WIDTH - MLA_HEADS * V_HEAD_DIM
CONV_WIDTH = 3
IN_SPLITS = (Q_LORA_RANK,
             Q_LORA_RANK + KV_LORA_RANK,
             Q_LORA_RANK + KV_LORA_RANK + QK_ROPE_DIM,
             Q_LORA_RANK + KV_LORA_RANK + QK_ROPE_DIM + CONV_DIM,
             Q_LORA_RANK + KV_LORA_RANK + QK_ROPE_DIM + 2 * CONV_DIM)
IN_COLS = Q_LORA_RANK + KV_LORA_RANK + QK_ROPE_DIM + 3 * CONV_DIM
RWKV_HEAD_SIZE = 64
RWKV_HEADS = D_MODEL // RWKV_HEAD_SIZE
DECAY_LORA = 128
AAA_LORA = 128
GATE_LORA = 480
GN_EPS = 64e-5
D_FF = 11008
N_SUB = 3
FFN_RES = 0.5
RMS_EPS = 1e-6

kernel_name = 'hybrid_mla_shortconv_rwkv7_adaln_decode_step'


def _rms_norm(x, g):
    xf = x.astype(jnp.float32)
    y = xf * lax.rsqrt(jnp.mean(xf * xf, axis=-1, keepdims=True) + RMS_EPS)
    return (y * g.astype(jnp.float32)).astype(x.dtype)


def _ada_norm(x, g, shift, scale):
    return _rms_norm(x, g) * (1 + scale[:, None, :]) + shift[:, None, :]


def _swiglu(h, w1, w3, w2):
    return (jax.nn.silu(h @ w1) * (h @ w3)) @ w2


def _apply_rope(x, pos):
    half = QK_ROPE_DIM // 2
    inv_freq = ROPE_THETA ** (-jnp.arange(half, dtype=jnp.float32) / half)
    ang = pos.astype(jnp.float32)[:, None] * inv_freq[None, :]
    ang = ang.reshape(ang.shape[:1] + (1,) * (x.ndim - 3) + (half,))
    cos, sin = jnp.cos(ang), jnp.sin(ang)
    x1 = x[..., :half].astype(jnp.float32)
    x2 = x[..., half:].astype(jnp.float32)
    return jnp.concatenate([x1 * cos - x2 * sin, x1 * sin + x2 * cos], axis=-1).astype(x.dtype)


def _latent_scores(q_lat, q_rope, ckv, kr):
    s = jnp.einsum('bthc,bkc->bthk', q_lat, ckv) + jnp.einsum('bthr,bkr->bthk', q_rope, kr)
    return s.astype(jnp.float32) * MLA_SCALE


def _mla_prompt_attend(q_lat, q_rope, ckv, kr):
    B, S, H, C = q_lat.shape
    nb = S // Q_BLOCK

    def to_blocks(t):
        return jnp.moveaxis(t.reshape((B, nb, Q_BLOCK) + t.shape[2:]), 1, 0)

    kpos = jnp.arange(S)

    def block(args):
        ql, qr, bi = args
        qpos = bi * Q_BLOCK + jnp.arange(Q_BLOCK)
        s = _latent_scores(ql, qr, ckv, kr)
        s = jnp.where((kpos[None, :] <= qpos[:, None])[None, :, None, :], s, -jnp.inf)
        p = jax.nn.softmax(s, axis=-1).astype(ckv.dtype)
        return jnp.einsum('bthk,bkc->bthc', p, ckv)

    o = lax.map(block, (to_blocks(q_lat), to_blocks(q_rope), jnp.arange(nb)))
    return jnp.moveaxis(o, 0, 1).reshape(B, S, H, C)


def _online_update(carry, s, vals):
    m, l, acc = carry
    m_new = jnp.maximum(m, s.max(axis=-1))
    alpha = jnp.exp(m - m_new)
    p = jnp.exp(s - m_new[..., None])
    l = l * alpha + p.sum(axis=-1)
    acc = acc * alpha[..., None] + jnp.einsum('bthk,bkc->bthc', p, vals.astype(jnp.float32))
    return (m_new, l, acc)


def _mla_sample_attend(q_lat, q_rope, cache_ckv, cache_kr, layer_idx, page_table, ckv_new, kr_new):
    B, T, H, C = q_lat.shape

    def page_step(carry, phys):
        ckv = cache_ckv[layer_idx, phys]
        kr = cache_kr[layer_idx, phys]
        return _online_update(carry, _latent_scores(q_lat, q_rope, ckv, kr), ckv), None

    init = (jnp.full((B, T, H), -jnp.inf, jnp.float32),
            jnp.zeros((B, T, H), jnp.float32),
            jnp.zeros((B, T, H, C), jnp.float32))
    carry, _ = lax.scan(page_step, init, page_table.T)
    tpos = jnp.arange(T)
    s_new = _latent_scores(q_lat, q_rope, ckv_new, kr_new)
    s_new = jnp.where((tpos[None, :] <= tpos[:, None])[None, :, None, :], s_new, -jnp.inf)
    m, l, acc = _online_update(carry, s_new, ckv_new)
    return (acc / l[..., None]).astype(q_lat.dtype)


def _short_conv(u, prev, w):
    T = u.shape[1]
    full = jnp.concatenate([prev.astype(u.dtype), u], axis=1)
    y = full[:, 0:T] * w[0]
    for tap in range(1, CONV_WIDTH):
        y = y + full[:, tap:tap + T] * w[tap]
    return y, full[:, T:]


def _mla_conv_mixer(h, pos, w_in, q_norm_g, kv_norm_g, w_uq, w_ukv, conv_w, w_out, past):
    B, T, _ = h.shape
    cq, ckv, kr, gate_b, gate_c, u = jnp.split(h @ w_in, IN_SPLITS, axis=-1)
    q = (_rms_norm(cq, q_norm_g) @ w_uq).reshape(B, T, MLA_HEADS, QK_NOPE_DIM + QK_ROPE_DIM)
    q_rope = _apply_rope(q[..., QK_NOPE_DIM:], pos)
    q_lat = jnp.einsum('bthn,chn->bthc', q[..., :QK_NOPE_DIM], w_ukv[..., :QK_NOPE_DIM])
    ckv = _rms_norm(ckv, kv_norm_g)
    kr = _apply_rope(kr, pos)
    if past is None:
        o_lat = _mla_prompt_attend(q_lat, q_rope, ckv, kr)
        conv_prev = jnp.zeros((B, CONV_WIDTH - 1, CONV_DIM), h.dtype)
    else:
        cache_ckv, cache_kr, layer_idx, page_table, conv_prev = past
        o_lat = _mla_sample_attend(q_lat, q_rope, cache_ckv, cache_kr, layer_idx, page_table, ckv, kr)
    o_att = jnp.einsum('bthc,chv->bthv', o_lat, w_ukv[..., QK_NOPE_DIM:]).reshape(B, T, MLA_HEADS * V_HEAD_DIM)
    conv_out, conv_new = _short_conv(gate_c * u, conv_prev, conv_w)
    y = jnp.concatenate([o_att, gate_b * conv_out], axis=-1) @ w_out
    return y, ckv, kr, conv_new


def _wkv7_step(S, inp):
    r, dec, k, v, a_vec, b_vec = inp
    sa = jnp.einsum('bhvk,bhk->bhv', S, a_vec)
    S = S * dec[:, :, None, :] + sa[..., None] * b_vec[:, :, None, :] + v[..., None] * k[:, :, None, :]
    return S, jnp.einsum('bhvk,bhk->bhv', S, r)


def _rwkv7_time_mix(h, prev_h, S0, mu, w0, w1, w2, a0, a1, a2, g1, g2, k_k, k_a, r_k,
                    w_r, w_k, w_v, w_o, ln_w, ln_b):
    B, T, D = h.shape
    f32 = jnp.float32
    xx = jnp.concatenate([prev_h[:, None, :].astype(h.dtype), h[:, :-1]], axis=1) - h
    xr, xw, xk, xv, xa, xg = (h + xx * mu[i] for i in range(6))
    r = xr @ w_r
    k = xk @ w_k
    v = xv @ w_v
    w_ll = -jax.nn.softplus(-(w0 + jnp.tanh(xw @ w1) @ w2).astype(f32)) - 0.5
    decay = jnp.exp(-jnp.exp(w_ll))
    a = jax.nn.sigmoid((a0 + (xa @ a1) @ a2).astype(f32))
    g = jax.nn.sigmoid(xg @ g1) @ g2

    def heads(t):
        return t.astype(f32).reshape(B, T, RWKV_HEADS, RWKV_HEAD_SIZE)

    rf, kf, vf, af, df = heads(r), heads(k), heads(v), heads(a), heads(decay)
    kk_w = k_k.astype(f32).reshape(RWKV_HEADS, RWKV_HEAD_SIZE)
    ka_w = k_a.astype(f32).reshape(RWKV_HEADS, RWKV_HEAD_SIZE)
    kk = kf * kk_w
    kk = kk * lax.rsqrt(jnp.maximum(jnp.sum(kk * kk, axis=-1, keepdims=True), 1e-24))
    kf = kf * (1 + (af - 1) * ka_w)

    def tm(t):
        return jnp.moveaxis(t, 1, 0)

    S, ys = lax.scan(_wkv7_step, S0.astype(f32), (tm(rf), tm(df), tm(kf), tm(vf), tm(-kk), tm(kk * af)))
    y = jnp.moveaxis(ys, 0, 1)
    mean = jnp.mean(y, axis=-1, keepdims=True)
    var = jnp.mean(jnp.square(y - mean), axis=-1, keepdims=True)
    y = ((y - mean) * lax.rsqrt(var + GN_EPS)).reshape(B, T, D) * ln_w + ln_b
    bonus = (jnp.sum(rf * kf * r_k.astype(f32), axis=-1, keepdims=True) * vf).reshape(B, T, D)
    out = ((y + bonus).astype(h.dtype) * g) @ w_o
    return out, h[:, -1], S.astype(h.dtype)


def setup_inputs(seed: int = 0) -> dict:
    key = jax.random.key(seed)
    ks = iter(jax.random.split(key, 64))
    f32 = jnp.float32
    D = D_MODEL

    def nrm(shape, scale):
        return jax.random.normal(next(ks), shape, f32) * scale

    n_pages = PAST_LEN // PAGE_SIZE
    n_used = DEC_BATCH * n_pages
    n_pool = n_used + n_used // 4
    page_table = jax.random.permutation(next(ks), n_pool)[:n_used].reshape(DEC_BATCH, n_pages).astype(jnp.int32)
    ada_s = 0.5 * D ** -0.5
    return {
        'x_prompt': nrm((BATCH, SEQ, D), 1.0),
        'x_sample': nrm((DEC_BATCH, DEC_SEQ, D), 1.0),
        'cache_kv_latent': nrm((N_MLA_LAYERS, n_pool, PAGE_SIZE, KV_LORA_RANK), 1.0),
        'cache_k_rope': nrm((N_MLA_LAYERS, n_pool, PAGE_SIZE, QK_ROPE_DIM), 1.0),
        'state_conv': nrm((N_MLA_LAYERS, DEC_BATCH, CONV_WIDTH - 1, CONV_DIM), 1.0),
        'state_rwkv_shift': nrm((N_RWKV_LAYERS, DEC_BATCH, D), 1.0),
        'state_rwkv_wkv': nrm((N_RWKV_LAYERS, DEC_BATCH, RWKV_HEADS, RWKV_HEAD_SIZE, RWKV_HEAD_SIZE), 0.3),
        'page_table': page_table,
        'c_prompt': nrm((BATCH, D), 1.0),
        'c_sample': nrm((DEC_BATCH, D), 1.0),
        'w_ada': nrm((DEPTH, D, N_SUB * 3 * D), ada_s),
        'b_ada': nrm((DEPTH, N_SUB * 3 * D), 0.01),
        'norm_g': 1.0 + nrm((DEPTH, N_SUB, D), 0.01),
        'ffn_w1': nrm((DEPTH, 2, D, D_FF), D ** -0.5),
        'ffn_w3': nrm((DEPTH, 2, D, D_FF), D ** -0.5),
        'ffn_w2': nrm((DEPTH, 2, D_FF, D), D_FF ** -0.5),
        'mla_w_in': nrm((N_MLA_LAYERS, D, IN_COLS), D ** -0.5),
        'mla_q_norm': 1.0 + nrm((N_MLA_LAYERS, Q_LORA_RANK), 0.01),
        'mla_kv_norm': 1.0 + nrm((N_MLA_LAYERS, KV_LORA_RANK), 0.01),
        'mla_w_uq': nrm((N_MLA_LAYERS, Q_LORA_RANK, MLA_HEADS * (QK_NOPE_DIM + QK_ROPE_DIM)), Q_LORA_RANK ** -0.5),
        'mla_w_ukv': nrm((N_MLA_LAYERS, KV_LORA_RANK, MLA_HEADS, QK_NOPE_DIM + V_HEAD_DIM), KV_LORA_RANK ** -0.5),
        'conv_w': nrm((N_MLA_LAYERS, CONV_WIDTH, CONV_DIM), CONV_WIDTH ** -0.5),
        'mix_w_out': nrm((N_MLA_LAYERS, MIX_WIDTH, D), MIX_WIDTH ** -0.5),
        'rwkv_mu': jax.random.uniform(next(ks), (N_RWKV_LAYERS, 6, D), f32),
        'rwkv_w0': -2.0 + nrm((N_RWKV_LAYERS, D), 0.5),
        'rwkv_w1': nrm((N_RWKV_LAYERS, D, DECAY_LORA), 0.1 * D ** -0.5),
        'rwkv_w2': nrm((N_RWKV_LAYERS, DECAY_LORA, D), 0.1 * DECAY_LORA ** -0.5),
        'rwkv_a0': nrm((N_RWKV_LAYERS, D), 0.1),
        'rwkv_a1': nrm((N_RWKV_LAYERS, D, AAA_LORA), 0.1 * D ** -0.5),
        'rwkv_a2': nrm((N_RWKV_LAYERS, AAA_LORA, D), 0.1 * AAA_LORA ** -0.5),
        'rwkv_g1': nrm((N_RWKV_LAYERS, D, GATE_LORA), D ** -0.5),
        'rwkv_g2': nrm((N_RWKV_LAYERS, GATE_LORA, D), GATE_LORA ** -0.5),
        'rwkv_k_k': 0.85 + nrm((N_RWKV_LAYERS, D), 0.05),
        'rwkv_k_a': 1.0 + nrm((N_RWKV_LAYERS, D), 0.05),
        'rwkv_r_k': nrm((N_RWKV_LAYERS, RWKV_HEADS, RWKV_HEAD_SIZE), 0.1),
        'rwkv_w_r': nrm((N_RWKV_LAYERS, D, D), D ** -0.5),
        'rwkv_w_k': nrm((N_RWKV_LAYERS, D, D), D ** -0.5),
        'rwkv_w_v': nrm((N_RWKV_LAYERS, D, D), D ** -0.5),
        'rwkv_w_o': nrm((N_RWKV_LAYERS, D, D), D ** -0.5),
        'rwkv_ln_w': 1.0 + nrm((N_RWKV_LAYERS, D), 0.01),
        'rwkv_ln_b': nrm((N_RWKV_LAYERS, D), 0.01),
        'final_norm_g': 1.0 + nrm((D,), 0.01),
    }


def reference(x_prompt, x_sample, cache_kv_latent, cache_k_rope, state_conv, state_rwkv_shift, state_rwkv_wkv,
              page_table, c_prompt, c_sample,
              w_ada, b_ada, norm_g, ffn_w1, ffn_w3, ffn_w2,
              mla_w_in, mla_q_norm, mla_kv_norm, mla_w_uq, mla_w_ukv, conv_w, mix_w_out,
              rwkv_mu, rwkv_w0, rwkv_w1, rwkv_w2, rwkv_a0, rwkv_a1, rwkv_a2, rwkv_g1, rwkv_g2,
              rwkv_k_k, rwkv_k_a, rwkv_r_k, rwkv_w_r, rwkv_w_k, rwkv_w_v, rwkv_w_o, rwkv_ln_w, rwkv_ln_b,
              final_norm_g):

    def run_group(x, c, pos, is_sample):
        n_seq = x.shape[0]
        cs = jax.nn.silu(c)
        kv_rows, kr_rows, conv_states, shift_states, wkv_states = [], [], [], [], []
        for layer in range(DEPTH):
            ada = (cs @ w_ada[layer] + b_ada[layer]).reshape(n_seq, N_SUB, 3, D_MODEL)
            shift, scale, gate = ada[:, :, 0], ada[:, :, 1], ada[:, :, 2]
            h = _ada_norm(x, norm_g[layer, 0], shift[:, 0], scale[:, 0])
            x = x + FFN_RES * gate[:, 0, None] * _swiglu(h, ffn_w1[layer, 0], ffn_w3[layer, 0], ffn_w2[layer, 0])
            h = _ada_norm(x, norm_g[layer, 1], shift[:, 1], scale[:, 1])
            if layer % 2 == 0:
                i = layer // 2
                past = (cache_kv_latent, cache_k_rope, i, page_table, state_conv[i]) if is_sample else None
                y, ckv, kr, conv_new = _mla_conv_mixer(h, pos, mla_w_in[i], mla_q_norm[i], mla_kv_norm[i],
                                                       mla_w_uq[i], mla_w_ukv[i], conv_w[i], mix_w_out[i], past)
                kv_rows.append(ckv)
                kr_rows.append(kr)
                conv_states.append(conv_new)
            else:
                j = layer // 2
                if is_sample:
                    prev_h, S0 = state_rwkv_shift[j], state_rwkv_wkv[j]
                else:
                    prev_h = jnp.zeros((n_seq, D_MODEL), x.dtype)
                    S0 = jnp.zeros((n_seq, RWKV_HEADS, RWKV_HEAD_SIZE, RWKV_HEAD_SIZE), jnp.float32)
                y, last_h, S = _rwkv7_time_mix(h, prev_h, S0, rwkv_mu[j], rwkv_w0[j], rwkv_w1[j], rwkv_w2[j],
                                               rwkv_a0[j], rwkv_a1[j], rwkv_a2[j], rwkv_g1[j], rwkv_g2[j],
                                               rwkv_k_k[j], rwkv_k_a[j], rwkv_r_k[j], rwkv_w_r[j], rwkv_w_k[j],
                                               rwkv_w_v[j], rwkv_w_o[j], rwkv_ln_w[j], rwkv_ln_b[j])
                shift_states.append(last_h)
                wkv_states.append(S)
            x = x + gate[:, 1, None] * y
            h = _ada_norm(x, norm_g[layer, 2], shift[:, 2], scale[:, 2])
            x = x + FFN_RES * gate[:, 2, None] * _swiglu(h, ffn_w1[layer, 1], ffn_w3[layer, 1], ffn_w2[layer, 1])
        return (_rms_norm(x, final_norm_g), jnp.stack(kv_rows), jnp.stack(kr_rows), jnp.stack(conv_states),
                jnp.stack(shift_states), jnp.stack(wkv_states))

    pos_prompt = jnp.arange(x_prompt.shape[1], dtype=jnp.int32)
    pos_sample = PAST_LEN + jnp.arange(x_sample.shape[1], dtype=jnp.int32)
    y_prompt, p_kv, p_kr, p_conv, p_shift, p_wkv = run_group(x_prompt, c_prompt, pos_prompt, False)
    y_sample, s_kv, s_kr, s_conv, s_shift, s_wkv = run_group(x_sample, c_sample, pos_sample, True)
    return (y_prompt, y_sample, p_kv, p_kr, p_conv, p_shift, p_wkv, s_kv, s_kr, s_conv, s_shift, s_wkv)
```

```python
import functools

import jax
import jax.numpy as jnp
from jax import lax
from jax.experimental import pallas as pl
from jax.experimental.pallas import tpu as pltpu

F32 = jnp.float32
BF16 = jnp.bfloat16

RMS_EPS = 1e-6
GN_EPS = 64e-5
ROPE_THETA = 10000.0
FFN_RES = 0.5
WKV_CHUNK = 64
WKV_LANES = 256
ATTN_TQ = 512
V7X_VMEM_LIMIT_BYTES = 56 * 1024 * 1024
HI = lax.Precision.HIGHEST


def _params(sem):
    return pltpu.CompilerParams(dimension_semantics=sem, vmem_limit_bytes=V7X_VMEM_LIMIT_BYTES)


def _dot(a, b, precision=None):
    return jnp.dot(a, b, preferred_element_type=F32, precision=precision)


def _dot_nt(a, b, precision=None):
    return lax.dot_general(a, b, (((1,), (1,)), ((), ())), preferred_element_type=F32, precision=precision)


def _dot_tn(a, b, precision=None):
    return lax.dot_general(a, b, (((0,), (0,)), ((), ())), preferred_element_type=F32, precision=precision)


def _sigmoid(x):
    return 1.0 / (1.0 + jnp.exp(-x))


def _silu(x):
    return x * _sigmoid(x)


def _w(arr, lead=(), roff=0, coff=0):
    return dict(arr=arr, lead=tuple(lead), roff=roff, coff=coff)


def _mm(xs, ws, accs, epi, out_dtypes, *, n, tm, tn, tk=None, extras=(), x_pro=None,
        out_shapes=None, out_specs=None, name):
    m, kdim = xs[0].shape
    tk = kdim if tk is None else tk
    assert m % tm == 0 and kdim % tk == 0, (name, m, tm, kdim, tk)
    nk = kdim // tk
    grid = (m // tm, pl.cdiv(n, tn), nk)
    nx, nw, ne, na, no = len(xs), len(ws), len(extras), len(accs), len(out_dtypes)

    in_specs = [pl.BlockSpec((tm, tk), lambda i, j, k: (i, k)) for _ in xs]
    for w in ws:
        nlead = len(w["lead"])
        assert w["arr"].ndim == nlead + 2

        def w_map(i, j, k, w=w):
            return w["lead"] + (w["roff"] + k, w["coff"] + j)

        in_specs.append(pl.BlockSpec((None,) * nlead + (tk, tn), w_map))
    for (_, bshape, imap) in extras:
        in_specs.append(pl.BlockSpec(bshape, imap))
    if out_shapes is None:
        out_shapes = [jax.ShapeDtypeStruct((m, n), dt) for dt in out_dtypes]
        out_specs = [pl.BlockSpec((tm, tn), lambda i, j, k: (i, j)) for _ in out_dtypes]

    def kern(*refs):
        x_refs = refs[:nx]
        w_refs = refs[nx:nx + nw]
        e_refs = refs[nx + nw:nx + nw + ne]
        o_refs = refs[nx + nw + ne:nx + nw + ne + no]
        acc_refs = refs[nx + nw + ne + no:]
        xv = []
        for r in x_refs:
            v = r[...]
            if x_pro is not None:
                v = x_pro(v.astype(F32))
            xv.append(v.astype(BF16))
        wv = [r[...].astype(BF16) for r in w_refs]
        parts = []
        for acc in accs:
            tot = None
            for (xi, wi) in acc:
                d = _dot(xv[xi], wv[wi])
                tot = d if tot is None else tot + d
            parts.append(tot)

        def finalize(vals):
            ev = []
            for r in e_refs:
                v = r[...]
                ev.append(v.reshape(v.shape[-2:]))
            outs = epi(vals, ev)
            for o_ref, o in zip(o_refs, outs):
                o_ref[...] = o.astype(o_ref.dtype).reshape(o_ref.shape)

        if nk == 1:
            finalize(parts)
        else:
            k = pl.program_id(2)

            @pl.when(k == 0)
            def _():
                for a_ref, p in zip(acc_refs, parts):
                    a_ref[...] = p

            @pl.when(k > 0)
            def _():
                for a_ref, p in zip(acc_refs, parts):
                    a_ref[...] += p

            @pl.when(k == nk - 1)
            def _():
                finalize([a_ref[...] for a_ref in acc_refs])

    scratch = [pltpu.VMEM((tm, tn), F32) for _ in range(na)] if nk > 1 else []
    args = list(xs) + [w["arr"] for w in ws] + [e[0] for e in extras]
    outs = pl.pallas_call(
        kern, out_shape=out_shapes, grid=grid, in_specs=in_specs, out_specs=out_specs,
        scratch_shapes=scratch, name=name,
        compiler_params=_params(("parallel", "parallel", "arbitrary")),
    )(*args)
    return outs


def _ex_row(arr, tm, tn, coff=0):
    return (arr, (tm, tn), lambda i, j, k: (i, coff + j))


def _ex_col(arr, tn, coff=0):
    return (arr, (1, tn), lambda i, j, k: (0, coff + j))


class _Mod:
    def __init__(self, ada, d, per_row, row0, t_len):
        self.d, self.per_row, self.row0, self.t_len = d, per_row, row0, t_len
        rows = ada.shape[0]
        self.a2 = ada
        self.a3 = ada.reshape(rows, 1, ada.shape[1])
        self.a3r = ada.reshape(1, rows, ada.shape[1])

    def norm_operand(self, sub, kind, tt):
        slab = sub * 3 + kind
        if self.per_row:
            return self.a3r, (1, tt, self.d), (lambda b, t: (0, t, slab))
        row0 = self.row0
        return self.a3, (1, 1, self.d), (lambda b, t: (row0 + b, 0, slab))

    def mm_extra(self, sub, kind, tm, tn):
        cb = (sub * 3 + kind) * (self.d // tn)
        if self.per_row:
            return (self.a2, (tm, tn), lambda i, j, k: (i, cb + j))
        row0, t_len = self.row0, self.t_len
        return (self.a3, (1, 1, tn), lambda i, j, k: (row0 + (i * tm) // t_len, 0, cb + j))


def _adanorm(x3, g, mod, sub, out_dtype, tt, name):
    b, t, d = x3.shape
    use_mod = mod is not None

    def kern(*refs):
        if use_mod:
            x_ref, g_ref, sh_ref, sc_ref, o_ref = refs
        else:
            x_ref, g_ref, o_ref = refs
        xf = x_ref[0]
        y = xf * lax.rsqrt(jnp.mean(xf * xf, axis=-1, keepdims=True) + RMS_EPS) * g_ref[...]
        if use_mod:
            y = y * (1.0 + sc_ref[0]) + sh_ref[0]
        o_ref[0] = y.astype(o_ref.dtype)

    in_specs = [pl.BlockSpec((1, tt, d), lambda bi, ti: (bi, ti, 0)),
                pl.BlockSpec((1, d), lambda bi, ti: (0, 0))]
    args = [x3, g.reshape(1, d)]
    if use_mod:
        for kind in (0, 1):
            arr, bshape, imap = mod.norm_operand(sub, kind, tt)
            in_specs.append(pl.BlockSpec(bshape, imap))
            args.append(arr)
    return pl.pallas_call(
        kern, out_shape=jax.ShapeDtypeStruct((b, t, d), out_dtype), grid=(b, t // tt),
        in_specs=in_specs, out_specs=pl.BlockSpec((1, tt, d), lambda bi, ti: (bi, ti, 0)),
        name=name, compiler_params=_params(("parallel", "parallel")),
    )(*args)


def _rope_tables(pos, half):
    inv_freq = ROPE_THETA ** (-jnp.arange(half, dtype=F32) / half)
    ang = pos.astype(F32)[:, None] * inv_freq[None, :]
    cos, sin = jnp.cos(ang), jnp.sin(ang)
    return jnp.concatenate([cos, cos, -sin, sin], axis=-1)


def _rope_cols(w, half):
    return jnp.concatenate([w, w[..., half:], w[..., :half]], axis=-1)


def _kvprep(a, kr_raw, tab, q_norm_g, kv_norm_g, ql, kvl, rd, tm, name):
    m = a.shape[0]

    def kern(a_ref, kr_ref, tab_ref, qg_ref, kg_ref, qn_ref, ckv_ref, kro_ref, kcat_ref):
        av = a_ref[...]
        cq, ck = av[:, :ql], av[:, ql:]
        qn = cq * lax.rsqrt(jnp.mean(cq * cq, axis=-1, keepdims=True) + RMS_EPS) * qg_ref[...]
        qn_ref[...] = qn.astype(BF16)
        ckv = ck * lax.rsqrt(jnp.mean(ck * ck, axis=-1, keepdims=True) + RMS_EPS) * kg_ref[...]
        ckv_ref[...] = ckv
        r = kr_ref[...] * tab_ref[...]
        rope = r + pltpu.roll(r, rd, 1)
        kro_ref[...] = rope[:, :rd]
        lane = lax.broadcasted_iota(jnp.int32, rope.shape, 1)
        kcat_ref[...] = jnp.concatenate([ckv, jnp.where(lane < rd, rope, 0.0)], axis=1).astype(BF16)

    return pl.pallas_call(
        kern,
        out_shape=[jax.ShapeDtypeStruct((m, ql), BF16), jax.ShapeDtypeStruct((m, kvl), F32),
                   jax.ShapeDtypeStruct((m, rd), F32), jax.ShapeDtypeStruct((m, kvl + 2 * rd), BF16)],
        grid=(m // tm,),
        in_specs=[pl.BlockSpec((tm, ql + kvl), lambda i: (i, 0)),
                  pl.BlockSpec((tm, 2 * rd), lambda i: (i, 0)),
                  pl.BlockSpec((tm, 2 * rd), lambda i: (i, 0)),
                  pl.BlockSpec((1, ql), lambda i: (0, 0)),
                  pl.BlockSpec((1, kvl), lambda i: (0, 0))],
        out_specs=[pl.BlockSpec((tm, ql), lambda i: (i, 0)), pl.BlockSpec((tm, kvl), lambda i: (i, 0)),
                   pl.BlockSpec((tm, rd), lambda i: (i, 0)), pl.BlockSpec((tm, kvl + 2 * rd), lambda i: (i, 0))],
        name=name, compiler_params=_params(("parallel",)),
    )(a, kr_raw, tab, q_norm_g.reshape(1, ql), kv_norm_g.reshape(1, kvl))


def _qproj(qn, w_nope, w_rope, w_uk_t, tab, tm, name):
    m, ql = qn.shape
    nh, _, nd = w_nope.shape
    kvl = w_uk_t.shape[2]
    rd2 = w_rope.shape[2]

    def kern(qn_ref, wn_ref, wr_ref, wuk_ref, tab_ref, o_ref):
        q = qn_ref[...]
        q_nope = _dot(q, wn_ref[...].astype(BF16))
        q_lat = _dot(q_nope.astype(BF16), wuk_ref[...].astype(BF16))
        r = _dot(q, wr_ref[...].astype(BF16)) * tab_ref[...]
        q_rope = r + pltpu.roll(r, rd2 // 2, 1)
        o_ref[...] = jnp.concatenate([q_lat, q_rope], axis=1).astype(BF16)

    return pl.pallas_call(
        kern, out_shape=jax.ShapeDtypeStruct((nh, m, kvl + rd2), BF16), grid=(nh, m // tm),
        in_specs=[pl.BlockSpec((tm, ql), lambda h, i: (i, 0)),
                  pl.BlockSpec((None, ql, nd), lambda h, i: (h, 0, 0)),
                  pl.BlockSpec((None, ql, rd2), lambda h, i: (h, 0, 0)),
                  pl.BlockSpec((None, nd, kvl), lambda h, i: (h, 0, 0)),
                  pl.BlockSpec((tm, rd2), lambda h, i: (i, 0))],
        out_specs=pl.BlockSpec((None, tm, kvl + rd2), lambda h, i: (h, i, 0)),
        name=name, compiler_params=_params(("parallel", "parallel")),
    )(qn, w_nope, w_rope, w_uk_t, tab)


def _prompt_attention(qcat, kcat, n_seq, t_len, kvl, scale, tq, name):
    nh, m, e = qcat.shape
    nq = t_len // tq

    def kern(q_ref, k_ref, o_ref, m_ref, l_ref, acc_ref):
        qi = pl.program_id(2)
        q = q_ref[...]
        m_ref[...] = jnp.full(m_ref.shape, -jnp.inf, F32)
        l_ref[...] = jnp.zeros(l_ref.shape, F32)
        acc_ref[...] = jnp.zeros(acc_ref.shape, F32)

        def update(j, masked):
            kb = k_ref[pl.ds(pl.multiple_of(j * tq, tq), tq), :]
            s = _dot_nt(q, kb) * scale
            if masked:
                row = lax.broadcasted_iota(jnp.int32, s.shape, 0)
                col = lax.broadcasted_iota(jnp.int32, s.shape, 1)
                s = jnp.where(col <= row, s, -jnp.inf)
            m_prev = m_ref[...]
            m_new = jnp.maximum(m_prev, jnp.max(s, axis=-1, keepdims=True))
            alpha = jnp.exp(m_prev - m_new)
            p = jnp.exp(s - m_new)
            l_ref[...] = l_ref[...] * alpha + jnp.sum(p, axis=-1, keepdims=True)
            acc_ref[...] = acc_ref[...] * alpha + _dot(p.astype(BF16), kb[:, :kvl])
            m_ref[...] = m_new

        def body(j, carry):
            update(j, False)
            return carry

        lax.fori_loop(0, qi, body, 0)
        update(qi, True)
        o_ref[...] = (acc_ref[...] / l_ref[...]).astype(o_ref.dtype)

    return pl.pallas_call(
        kern, out_shape=jax.ShapeDtypeStruct((nh, m, kvl), BF16), grid=(n_seq, nh, nq),
        in_specs=[pl.BlockSpec((None, tq, e), lambda b, h, qi: (h, b * nq + qi, 0)),
                  pl.BlockSpec((t_len, e), lambda b, h, qi: (b, 0))],
        out_specs=pl.BlockSpec((None, tq, kvl), lambda b, h, qi: (h, b * nq + qi, 0)),
        scratch_shapes=[pltpu.VMEM((tq, 1), F32), pltpu.VMEM((tq, 1), F32), pltpu.VMEM((tq, kvl), F32)],
        name=name, compiler_params=_params(("parallel", "parallel", "arbitrary")),
    )(qcat, kcat)


def _decode_attention(q, cache_kv, cache_kr, layer, page_table, ckv_new, kr_new, scale, pp, name):
    bsz, nh, e = q.shape
    page, kvl = cache_kv.shape[2], cache_kv.shape[3]
    rd = cache_kr.shape[3]
    n_pages = page_table.shape[1]
    assert n_pages % pp == 0
    ns = n_pages // pp

    def kern(pt_ref, q_ref, *refs):
        kv_refs = refs[:pp]
        kr_refs = refs[pp:2 * pp]
        cn_ref, rn_ref, o_ref, m_ref, l_ref, acc_ref = refs[2 * pp:]
        s_idx = pl.program_id(1)

        @pl.when(s_idx == 0)
        def _():
            m_ref[...] = jnp.full(m_ref.shape, -jnp.inf, F32)
            l_ref[...] = jnp.zeros(l_ref.shape, F32)
            acc_ref[...] = jnp.zeros(acc_ref.shape, F32)

        qv = q_ref[0]
        ql, qr = qv[:, :kvl], qv[:, kvl:kvl + rd]

        def online(s, pv_fn):
            m_prev = m_ref[...]
            m_new = jnp.maximum(m_prev, jnp.max(s, axis=-1, keepdims=True))
            alpha = jnp.exp(m_prev - m_new)
            p = jnp.exp(s - m_new)
            l_ref[...] = l_ref[...] * alpha + jnp.sum(p, axis=-1, keepdims=True)
            acc_ref[...] = acc_ref[...] * alpha + pv_fn(p)
            m_ref[...] = m_new

        for j in range(pp):
            kv = kv_refs[j][...].astype(BF16)
            kr = kr_refs[j][...].astype(BF16)
            s = (_dot_nt(ql, kv) + _dot_nt(qr, kr)) * scale
            online(s, lambda p, kv=kv: _dot(p.astype(BF16), kv))

        @pl.when(s_idx == ns - 1)
        def _():
            cn = cn_ref[0].astype(BF16).astype(F32)
            rn = rn_ref[0].astype(BF16).astype(F32)
            s = (jnp.sum(ql.astype(F32) * cn, axis=-1, keepdims=True)
                 + jnp.sum(qr.astype(F32) * rn, axis=-1, keepdims=True)) * scale
            online(s, lambda p: p.astype(BF16).astype(F32) * cn)
            o_ref[0] = acc_ref[...] / l_ref[...]

    in_specs = [pl.BlockSpec((1, nh, e), lambda b, s, pt: (b, 0, 0))]
    for j in range(pp):
        in_specs.append(pl.BlockSpec((None, None, page, kvl),
                                     lambda b, s, pt, j=j: (layer, pt[b, s * pp + j], 0, 0)))
    for j in range(pp):
        in_specs.append(pl.BlockSpec((None, None, page, rd),
                                     lambda b, s, pt, j=j: (layer, pt[b, s * pp + j], 0, 0)))
    in_specs.append(pl.BlockSpec((1, 1, kvl), lambda b, s, pt: (b, 0, 0)))
    in_specs.append(pl.BlockSpec((1, 1, rd), lambda b, s, pt: (b, 0, 0)))
    grid_spec = pltpu.PrefetchScalarGridSpec(
        num_scalar_prefetch=1, grid=(bsz, ns), in_specs=in_specs,
        out_specs=pl.BlockSpec((1, nh, kvl), lambda b, s, pt: (b, 0, 0)),
        scratch_shapes=[pltpu.VMEM((nh, 1), F32), pltpu.VMEM((nh, 1), F32), pltpu.VMEM((nh, kvl), F32)])
    return pl.pallas_call(
        kern, out_shape=jax.ShapeDtypeStruct((bsz, nh, kvl), F32), grid_spec=grid_spec, name=name,
        compiler_params=_params(("parallel", "arbitrary")),
    )(page_table, q, *([cache_kv] * pp), *([cache_kr] * pp), ckv_new, kr_new)


def _oproj(o_lat, w_uv, tm, name):
    nh, m, kvl = o_lat.shape
    vd = w_uv.shape[2]

    def kern(o_ref, w_ref, y_ref):
        y_ref[...] = _dot(o_ref[...].astype(BF16), w_ref[...].astype(BF16)).astype(BF16)

    return pl.pallas_call(
        kern, out_shape=jax.ShapeDtypeStruct((m, nh * vd), BF16), grid=(nh, m // tm),
        in_specs=[pl.BlockSpec((None, tm, kvl), lambda h, i: (h, i, 0)),
                  pl.BlockSpec((None, kvl, vd), lambda h, i: (h, 0, 0))],
        out_specs=pl.BlockSpec((tm, vd), lambda h, i: (i, h)),
        name=name, compiler_params=_params(("parallel", "parallel")),
    )(o_lat, w_uv)


def _conv_prompt(bcu, conv_w, n_seq, t_len, cd, tc, name):
    m = bcu.shape[0]
    ncb = cd // tc
    cw = conv_w.shape[0]

    def kern(b_ref, c_ref, u_ref, w_ref, y_ref, st_ref):
        xg = c_ref[...] * u_ref[...]
        w = w_ref[...]
        row = lax.broadcasted_iota(jnp.int32, xg.shape, 0)
        y = xg * w[cw - 1:cw]
        for d in range(1, cw):
            sh = jnp.where(row >= d, pltpu.roll(xg, d, 0), 0.0)
            y = y + sh * w[cw - 1 - d:cw - d]
        y_ref[...] = (b_ref[...] * y).astype(BF16)
        st_ref[0] = xg[t_len - (cw - 1):, :]

    return pl.pallas_call(
        kern, out_shape=[jax.ShapeDtypeStruct((m, cd), BF16), jax.ShapeDtypeStruct((n_seq, cw - 1, cd), F32)],
        grid=(n_seq, ncb),
        in_specs=[pl.BlockSpec((t_len, tc), lambda b, j: (b, j)),
                  pl.BlockSpec((t_len, tc), lambda b, j: (b, ncb + j)),
                  pl.BlockSpec((t_len, tc), lambda b, j: (b, 2 * ncb + j)),
                  pl.BlockSpec((cw, tc), lambda b, j: (0, j))],
        out_specs=[pl.BlockSpec((t_len, tc), lambda b, j: (b, j)),
                   pl.BlockSpec((1, cw - 1, tc), lambda b, j: (b, 0, j))],
        name=name, compiler_params=_params(("parallel", "parallel")),
    )(bcu, bcu, bcu, conv_w)


def _conv_step(bcu, prev, conv_w, cd, tc, name):
    bsz = bcu.shape[0]
    ncb = cd // tc
    cw = conv_w.shape[0]

    def kern(*refs):
        b_ref, c_ref, u_ref, w_ref = refs[:4]
        p_refs = refs[4:4 + cw - 1]
        y_ref = refs[4 + cw - 1]
        st_refs = refs[5 + cw - 1:]
        xg = c_ref[...] * u_ref[...]
        w = w_ref[...]
        y = xg * w[cw - 1:cw]
        for d in range(cw - 1):
            y = y + p_refs[d][...] * w[d:d + 1]
        y_ref[...] = (b_ref[...] * y).astype(BF16)
        for d in range(cw - 2):
            st_refs[d][...] = p_refs[d + 1][...]
        st_refs[cw - 2][...] = xg

    in_specs = [pl.BlockSpec((bsz, tc), lambda j: (0, j)),
                pl.BlockSpec((bsz, tc), lambda j: (0, ncb + j)),
                pl.BlockSpec((bsz, tc), lambda j: (0, 2 * ncb + j)),
                pl.BlockSpec((cw, tc), lambda j: (0, j))]
    for d in range(cw - 1):
        in_specs.append(pl.BlockSpec((bsz, tc), lambda j, d=d: (0, d * ncb + j)))
    outs = pl.pallas_call(
        kern,
        out_shape=[jax.ShapeDtypeStruct((bsz, cd), BF16)] + [jax.ShapeDtypeStruct((bsz, cd), F32)] * (cw - 1),
        grid=(ncb,), in_specs=in_specs,
        out_specs=[pl.BlockSpec((bsz, tc), lambda j: (0, j))] * cw,
        name=name, compiler_params=_params(("parallel",)),
    )(bcu, bcu, bcu, conv_w, *([prev] * (cw - 1)))
    return outs[0], jnp.stack(outs[1:], axis=1)


def _token_mix(h3, prev3, mu, tc, name):
    b, t, d = h3.shape
    nmix = mu.shape[0]
    shift_here = prev3 is None

    def kern(*refs):
        if shift_here:
            h_ref, mu_ref = refs[:2]
            o_refs = refs[2:]
        else:
            h_ref, p_ref, mu_ref = refs[:3]
            o_refs = refs[3:]
        hh = h_ref[0]
        if shift_here:
            row = lax.broadcasted_iota(jnp.int32, hh.shape, 0)
            prev = jnp.where(row >= 1, pltpu.roll(hh, 1, 0), 0.0)
        else:
            prev = p_ref[0]
        xx = prev - hh
        mv = mu_ref[...]
        for i in range(nmix):
            o_refs[i][0] = (hh + xx * mv[i:i + 1]).astype(BF16)

    blk = pl.BlockSpec((1, t, tc), lambda bi, j: (bi, 0, j))
    in_specs = [blk] + ([] if shift_here else [blk]) + [pl.BlockSpec((nmix, tc), lambda bi, j: (0, j))]
    args = [h3] + ([] if shift_here else [prev3]) + [mu]
    return pl.pallas_call(
        kern, out_shape=[jax.ShapeDtypeStruct((b, t, d), BF16)] * nmix, grid=(b, d // tc),
        in_specs=in_specs, out_specs=[blk] * nmix,
        name=name, compiler_params=_params(("parallel", "parallel")),
    )(*args)


def _wkv_head_prep(r, k, a, kkw, kaw):
    kk = k * kkw
    kk = kk * lax.rsqrt(jnp.maximum(jnp.sum(kk * kk, axis=-1, keepdims=True), 1e-24))
    kmod = k * (1.0 + (a - 1.0) * kaw)
    return kk, kmod


def _wkv_head_post(y, r, kmod, v, g, rkw, lnw, lnb):
    mean = jnp.mean(y, axis=-1, keepdims=True)
    var = jnp.mean(jnp.square(y - mean), axis=-1, keepdims=True)
    yn = (y - mean) * lax.rsqrt(var + GN_EPS) * lnw + lnb
    bonus = jnp.sum(r * kmod * rkw, axis=-1, keepdims=True) * v
    return (yn + bonus) * g


def _wkv_prompt(r, k, v, a, lw, g, k_k, k_a, r_k, ln_w, ln_b, n_seq, t_len, hs, name):
    m, d = r.shape
    c = WKV_CHUNK
    hl = WKV_LANES
    hpb = hl // hs
    nh = d // hs
    nc = t_len // c

    def kern(r_ref, k_ref, v_ref, a_ref, lw_ref, g_ref, kk_ref, ka_ref, rk_ref, lnw_ref, lnb_ref,
             o_ref, s_ref):
        ci = pl.program_id(2)

        @pl.when(ci == 0)
        def _():
            s_ref[...] = jnp.zeros(s_ref.shape, F32)

        ri = lax.broadcasted_iota(jnp.int32, (c, c), 0)
        cj = lax.broadcasted_iota(jnp.int32, (c, c), 1)
        incl = cj <= ri
        strict = cj < ri
        tri = jnp.where(incl, 1.0, 0.0).astype(F32)
        eye = jnp.where(cj == ri, 1.0, 0.0).astype(F32)
        for hh in range(hpb):
            sl = slice(hh * hs, (hh + 1) * hs)
            rr, k0, vv, aa, lwh, gg = (x[:, sl] for x in (r_ref, k_ref, v_ref, a_ref, lw_ref, g_ref))
            kkw, kaw, rkw, lnw, lnb = (x[:, sl] for x in (kk_ref, ka_ref, rk_ref, lnw_ref, lnb_ref))
            kk, kmod = _wkv_head_prep(rr, k0, aa, kkw, kaw)
            avec, bvec = -kk, kk * aa
            cum = _dot(tri, lwh, HI)
            cum_ex = cum - lwh
            tot = cum[c - 1:c, :]
            at = avec * jnp.exp(cum_ex)
            rt = rr * jnp.exp(cum)
            e_neg = jnp.exp(-cum)
            kt, bt = kmod * e_neg, bvec * e_neg
            e_tot = jnp.exp(tot - cum)
            kh, bh = kmod * e_tot, bvec * e_tot
            a_ab = jnp.where(strict, _dot_nt(at, bt, HI), 0.0)
            a_ak = jnp.where(strict, _dot_nt(at, kt, HI), 0.0)
            a_rb = jnp.where(incl, _dot_nt(rt, bt, HI), 0.0)
            a_rk = jnp.where(incl, _dot_nt(rt, kt, HI), 0.0)
            tinv = eye + a_ab
            lp = a_ab
            n_done = 2
            while n_done < c:
                lp = _dot(lp, lp, HI)
                tinv = tinv + _dot(tinv, lp, HI)
                n_done *= 2
            s0 = s_ref[0, hh]
            p = _dot(tinv, _dot_nt(at, s0, HI) + _dot(a_ak, vv, HI), HI)
            y = _dot_nt(rt, s0, HI) + _dot(a_rb, p, HI) + _dot(a_rk, vv, HI)
            s_ref[0, hh] = s0 * jnp.exp(tot) + _dot_tn(p, bh, HI) + _dot_tn(vv, kh, HI)
            o_ref[:, sl] = _wkv_head_post(y, rr, kmod, vv, gg, rkw, lnw, lnb).astype(BF16)

    blk = pl.BlockSpec((c, hl), lambda b, hg, ci: (b * nc + ci, hg))
    wblk = pl.BlockSpec((1, hl), lambda b, hg, ci: (0, hg))
    return pl.pallas_call(
        kern, out_shape=[jax.ShapeDtypeStruct((m, d), BF16), jax.ShapeDtypeStruct((n_seq, nh, hs, hs), F32)],
        grid=(n_seq, d // hl, nc),
        in_specs=[blk] * 6 + [wblk] * 5,
        out_specs=[blk, pl.BlockSpec((1, hpb, hs, hs), lambda b, hg, ci: (b, hg, 0, 0))],
        name=name, compiler_params=_params(("parallel", "parallel", "arbitrary")),
    )(r, k, v, a, lw, g, k_k, k_a, r_k, ln_w, ln_b)


def _wkv_step(r, k, v, a, lw, g, s0, k_k, k_a, r_k, ln_w, ln_b, bb, hb, name):
    bsz, nh, _, hs = r.shape

    def kern(r_ref, k_ref, v_ref, a_ref, lw_ref, g_ref, s_ref, kk_ref, ka_ref, rk_ref, lnw_ref, lnb_ref,
             o_ref, so_ref):
        rr, k0, vv, aa, gg = r_ref[...], k_ref[...], v_ref[...], a_ref[...], g_ref[...]
        dec = jnp.exp(lw_ref[...])
        kk, kmod = _wkv_head_prep(rr, k0, aa, kk_ref[...], ka_ref[...])
        avec, bvec = -kk, kk * aa
        s = s_ref[...]
        ri = lax.broadcasted_iota(jnp.int32, (hs, hs), 0)
        cj = lax.broadcasted_iota(jnp.int32, (hs, hs), 1)
        eye = jnp.where(ri == cj, 1.0, 0.0).astype(F32)
        sa = jnp.sum(s * avec, axis=-1, keepdims=True)
        vcol = jnp.sum(eye * vv, axis=-1, keepdims=True)
        s1 = s * dec + sa * bvec + vcol * kmod
        so_ref[...] = s1
        ycol = jnp.sum(s1 * rr, axis=-1, keepdims=True)
        y = jnp.sum(eye * ycol, axis=-2, keepdims=True)
        o_ref[...] = _wkv_head_post(y, rr, kmod, vv, gg, rk_ref[...], lnw_ref[...], lnb_ref[...])

    vblk = pl.BlockSpec((bb, hb, 1, hs), lambda i, j: (i, j, 0, 0))
    sblk = pl.BlockSpec((bb, hb, hs, hs), lambda i, j: (i, j, 0, 0))
    wblk = pl.BlockSpec((1, hb, 1, hs), lambda i, j: (0, j, 0, 0))
    return pl.pallas_call(
        kern, out_shape=[jax.ShapeDtypeStruct((bsz, nh, 1, hs), F32), jax.ShapeDtypeStruct((bsz, nh, hs, hs), F32)],
        grid=(bsz // bb, nh // hb),
        in_specs=[vblk] * 6 + [sblk] + [wblk] * 5,
        out_specs=[vblk, sblk],
        name=name, compiler_params=_params(("parallel", "parallel")),
    )(r, k, v, a, lw, g, s0, k_k, k_a, r_k, ln_w, ln_b)


def _pick(n, candidates):
    for c in candidates:
        if n % c == 0:
            return c
    return n


def kernel(x_prompt, x_sample, cache_kv_latent, cache_k_rope, state_conv, state_rwkv_shift, state_rwkv_wkv, page_table, c_prompt, c_sample, w_ada, b_ada, norm_g, ffn_w1, ffn_w3, ffn_w2, mla_w_in, mla_q_norm, mla_kv_norm, mla_w_uq, mla_w_ukv, conv_w, mix_w_out, rwkv_mu, rwkv_w0, rwkv_w1, rwkv_w2, rwkv_a0, rwkv_a1, rwkv_a2, rwkv_g1, rwkv_g2, rwkv_k_k, rwkv_k_a, rwkv_r_k, rwkv_w_r, rwkv_w_k, rwkv_w_v, rwkv_w_o, rwkv_ln_w, rwkv_ln_b, final_norm_g):
    nb, seq, d = x_prompt.shape
    nsb, dec_seq, _ = x_sample.shape
    assert dec_seq == 1
    depth = w_ada.shape[0]
    dff = ffn_w1.shape[-1]
    ql = mla_q_norm.shape[1]
    kvl = mla_kv_norm.shape[1]
    rd = cache_k_rope.shape[-1]
    nh_mla = mla_w_ukv.shape[2]
    nd = mla_w_uq.shape[2] // nh_mla - rd
    vd = mla_w_ukv.shape[3] - nd
    cd = conv_w.shape[-1]
    cw = conv_w.shape[1]
    hs = state_rwkv_wkv.shape[-1]
    nh_rwkv = d // hs
    past_len = page_table.shape[1] * cache_kv_latent.shape[2]
    mla_scale = float((nd + rd) ** -0.5)
    mp = nb * seq

    n_c = nsb + nb
    n_c_pad = -(-n_c // 16) * 16
    c_all = jnp.concatenate([c_sample, c_prompt, jnp.zeros((n_c_pad - n_c, d), F32)], axis=0)
    t_ada = _pick(9 * d, (512, 256, 128))
    adas = []
    for layer in range(depth):
        (ada,) = _mm([c_all], [_w(w_ada, (layer,))], [[(0, 0)]],
                     lambda acc, ev: (acc[0] + ev[0],), [F32],
                     n=9 * d, tm=n_c_pad, tn=t_ada, x_pro=_silu,
                     extras=[_ex_col(b_ada[layer].reshape(1, 9 * d), t_ada)], name=f"ada_l{layer}")
        adas.append(ada)

    def run_group(x2, is_sample):
        m = x2.shape[0]
        if is_sample:
            n_seq, t_len = m, 1
            xshape = (1, m, d)
            tm = m
            tt = m
            pos = jnp.full((m,), past_len, jnp.int32)
        else:
            n_seq, t_len = nb, seq
            xshape = (nb, seq, d)
            tm = _pick(seq, (1024, 512, 256, 128))
            tt = _pick(seq, (256, 128))
            pos = jnp.tile(jnp.arange(seq, dtype=jnp.int32), nb)
        tn = _pick(d, (512, 256, 128))
        tab = _rope_tables(pos, rd // 2)
        kv_rows, kr_rows, conv_states, shift_states, wkv_states = [], [], [], [], []

        def ffn(x2, mod, layer, idx, sub, tag):
            h = _adanorm(x2.reshape(xshape), norm_g[layer, sub], mod, sub, BF16, tt, f"norm_{tag}").reshape(m, d)
            tnf = _pick(dff, (256, 128))
            (gact,) = _mm([h], [_w(ffn_w1, (layer, idx)), _w(ffn_w3, (layer, idx))], [[(0, 0)], [(0, 1)]],
                          lambda acc, ev: (_silu(acc[0]) * acc[1],), [BF16],
                          n=dff, tm=tm, tn=tnf, name=f"ffn_up_{tag}")
            tkd = _pick(dff, (256, 128)) if not is_sample else dff
            tnd = _pick(d, (1024, 512, 256, 128)) if not is_sample else _pick(d, (256, 128))
            (xo,) = _mm([gact], [_w(ffn_w2, (layer, idx))], [[(0, 0)]],
                        lambda acc, ev: (ev[0] + FFN_RES * ev[1] * acc[0],), [F32],
                        n=d, tm=tm, tn=tnd, tk=tkd,
                        extras=[_ex_row(x2, tm, tnd), mod.mm_extra(sub, 2, tm, tnd)], name=f"ffn_down_{tag}")
            return xo

        for layer in range(depth):
            tag = f"{'s' if is_sample else 'p'}{layer}"
            mod = _Mod(adas[layer], d, is_sample, nsb, t_len)
            x2 = ffn(x2, mod, layer, 0, 0, tag + "a")
            if layer % 2 == 0:
                i = layer // 2
                h = _adanorm(x2.reshape(xshape), norm_g[layer, 1], mod, 1, BF16, tt, f"norm_mix_{tag}").reshape(m, d)
                w_in = mla_w_in[i]
                n_a = ql + kvl
                (a_proj,) = _mm([h], [_w(w_in)], [[(0, 0)]], lambda acc, ev: (acc[0],), [F32],
                                n=n_a, tm=tm, tn=_pick(n_a, (512, 256, 128)), name=f"mla_in_a_{tag}")
                w_kr = _rope_cols(w_in[:, n_a:n_a + rd], rd // 2)
                (kr_raw,) = _mm([h], [_w(w_kr)], [[(0, 0)]], lambda acc, ev: (acc[0],), [F32],
                                n=2 * rd, tm=tm, tn=2 * rd, name=f"mla_in_kr_{tag}")
                w_bcu = w_in[:, n_a + rd:]
                (bcu,) = _mm([h], [_w(w_bcu)], [[(0, 0)]], lambda acc, ev: (acc[0],), [F32],
                             n=3 * cd, tm=tm, tn=_pick(cd, (512, 256, 128)), name=f"mla_in_bcu_{tag}")
                qn, ckv, kr, kcat = _kvprep(a_proj, kr_raw, tab, mla_q_norm[i], mla_kv_norm[i], ql, kvl, rd,
                                            _pick(m, (256, 128)), f"kvprep_{tag}")
                w_uq = mla_w_uq[i].reshape(ql, nh_mla, nd + rd)
                w_nope = jnp.transpose(w_uq[:, :, :nd], (1, 0, 2))
                w_rope = _rope_cols(jnp.transpose(w_uq[:, :, nd:], (1, 0, 2)), rd // 2)
                w_uk_t = jnp.transpose(mla_w_ukv[i][:, :, :nd], (1, 2, 0))
                w_uv = jnp.transpose(mla_w_ukv[i][:, :, nd:], (1, 0, 2))
                qcat = _qproj(qn, w_nope, w_rope, w_uk_t, tab, _pick(m, (512, 256, 128)), f"qproj_{tag}")
                if is_sample:
                    o_lat = _decode_attention(jnp.transpose(qcat, (1, 0, 2)), cache_kv_latent, cache_k_rope, i,
                                              page_table, ckv.reshape(m, 1, kvl), kr.reshape(m, 1, rd),
                                              mla_scale, _pick(page_table.shape[1], (8, 4, 2, 1)), f"attn_{tag}")
                    o_lat = jnp.transpose(o_lat, (1, 0, 2))
                    prev = state_conv[i].reshape(m, (cw - 1) * cd)
                    conv_y, conv_new = _conv_step(bcu, prev, conv_w[i], cd, _pick(cd, (512, 256, 128)), f"conv_{tag}")
                else:
                    o_lat = _prompt_attention(qcat, kcat, n_seq, t_len, kvl, mla_scale,
                                              _pick(t_len, (ATTN_TQ, 128)), f"attn_{tag}")
                    conv_y, conv_new = _conv_prompt(bcu, conv_w[i], n_seq, t_len, cd, _pick(cd, (256, 128)),
                                                    f"conv_{tag}")
                o_att = _oproj(o_lat, w_uv, _pick(m, (512, 256, 128)), f"oproj_{tag}")
                assert nh_mla * vd == cd
                (x2,) = _mm([o_att, conv_y], [_w(mix_w_out, (i,)), _w(mix_w_out, (i,), roff=1)], [[(0, 0), (1, 1)]],
                            lambda acc, ev: (ev[0] + ev[1] * acc[0],), [F32],
                            n=d, tm=tm, tn=tn, extras=[_ex_row(x2, tm, tn), mod.mm_extra(1, 2, tm, tn)],
                            name=f"mix_out_{tag}")
                kv_rows.append(ckv.reshape(n_seq, t_len, kvl))
                kr_rows.append(kr.reshape(n_seq, t_len, rd))
                conv_states.append(conv_new)
            else:
                j = layer // 2
                h3 = _adanorm(x2.reshape(xshape), norm_g[layer, 1], mod, 1, F32, tt, f"norm_mix_{tag}")
                tcm = _pick(d, (256, 128))
                if is_sample:
                    mixes = _token_mix(h3, state_rwkv_shift[j].reshape(xshape), rwkv_mu[j], tcm, f"tmix_{tag}")
                    shift_states.append(h3.reshape(m, d))
                else:
                    mixes = _token_mix(h3, None, rwkv_mu[j], tcm, f"tmix_{tag}")
                    shift_states.append(h3[:, -1])
                xr, xw, xk, xv, xa, xg = (t.reshape(m, d) for t in mixes)

                def proj(x, w, tag2):
                    (o,) = _mm([x], [_w(w, (j,))], [[(0, 0)]], lambda acc, ev: (acc[0],), [F32],
                               n=d, tm=tm, tn=tn, name=f"rwkv_{tag2}_{tag}")
                    return o

                r, k, v = proj(xr, rwkv_w_r, "r"), proj(xk, rwkv_w_k, "k"), proj(xv, rwkv_w_v, "v")
                dl = rwkv_w1.shape[-1]
                (t1,) = _mm([xw], [_w(rwkv_w1, (j,))], [[(0, 0)]], lambda acc, ev: (jnp.tanh(acc[0]),), [BF16],
                            n=dl, tm=tm, tn=dl, name=f"rwkv_w1_{tag}")

                def decay_epi(acc, ev):
                    z = -(ev[0] + acc[0])
                    softplus = jnp.maximum(z, 0.0) + jnp.log(1.0 + jnp.exp(-jnp.abs(z)))
                    return (-jnp.exp(-softplus - 0.5),)

                (lw,) = _mm([t1], [_w(rwkv_w2, (j,))], [[(0, 0)]], decay_epi, [F32],
                            n=d, tm=tm, tn=tn, extras=[_ex_col(rwkv_w0[j].reshape(1, d), tn)], name=f"rwkv_w2_{tag}")
                al = rwkv_a1.shape[-1]
                (t2,) = _mm([xa], [_w(rwkv_a1, (j,))], [[(0, 0)]], lambda acc, ev: (acc[0],), [BF16],
                            n=al, tm=tm, tn=al, name=f"rwkv_a1_{tag}")
                (a,) = _mm([t2], [_w(rwkv_a2, (j,))], [[(0, 0)]], lambda acc, ev: (_sigmoid(ev[0] + acc[0]),), [F32],
                           n=d, tm=tm, tn=tn, extras=[_ex_col(rwkv_a0[j].reshape(1, d), tn)], name=f"rwkv_a2_{tag}")
                gl = rwkv_g1.shape[-1]
                glp = -(-gl // 128) * 128
                g1p = jnp.pad(rwkv_g1[j], ((0, 0), (0, glp - gl)))
                g2p = jnp.pad(rwkv_g2[j], ((0, glp - gl), (0, 0)))
                (t3,) = _mm([xg], [_w(g1p)], [[(0, 0)]], lambda acc, ev: (_sigmoid(acc[0]),), [BF16],
                            n=glp, tm=tm, tn=glp, name=f"rwkv_g1_{tag}")
                (g,) = _mm([t3], [_w(g2p)], [[(0, 0)]], lambda acc, ev: (acc[0],), [F32],
                           n=d, tm=tm, tn=tn, name=f"rwkv_g2_{tag}")
                if is_sample:
                    v4 = lambda t: t.reshape(m, nh_rwkv, 1, hs)
                    w4 = lambda t: t.reshape(1, nh_rwkv, 1, hs)
                    o4, s_new = _wkv_step(v4(r), v4(k), v4(v), v4(a), v4(lw), v4(g), state_rwkv_wkv[j],
                                          w4(rwkv_k_k[j]), w4(rwkv_k_a[j]), w4(rwkv_r_k[j]),
                                          w4(rwkv_ln_w[j]), w4(rwkv_ln_b[j]),
                                          _pick(m, (8, 4, 2, 1)), _pick(nh_rwkv, (16, 8, 4, 2, 1)), f"wkv_{tag}")
                    o_pre = o4.reshape(m, d)
                else:
                    w2d = lambda t: t.reshape(1, d)
                    o_pre, s_new = _wkv_prompt(r, k, v, a, lw, g, w2d(rwkv_k_k[j]), w2d(rwkv_k_a[j]),
                                               w2d(rwkv_r_k[j]), w2d(rwkv_ln_w[j]), w2d(rwkv_ln_b[j]),
                                               n_seq, t_len, hs, f"wkv_{tag}")
                (x2,) = _mm([o_pre], [_w(rwkv_w_o, (j,))], [[(0, 0)]],
                            lambda acc, ev: (ev[0] + ev[1] * acc[0],), [F32],
                            n=d, tm=tm, tn=tn, extras=[_ex_row(x2, tm, tn), mod.mm_extra(1, 2, tm, tn)],
                            name=f"rwkv_o_{tag}")
                wkv_states.append(s_new)
            x2 = ffn(x2, mod, layer, 1, 2, tag + "b")
        y = _adanorm(x2.reshape(xshape), final_norm_g, None, 0, F32, tt, f"final_norm_{'s' if is_sample else 'p'}")
        return (y.reshape(n_seq, t_len, d), jnp.stack(kv_rows), jnp.stack(kr_rows), jnp.stack(conv_states),
                jnp.stack(shift_states), jnp.stack(wkv_states))

    y_p, p_kv, p_kr, p_conv, p_shift, p_wkv = run_group(x_prompt.reshape(mp, d), False)
    y_s, s_kv, s_kr, s_conv, s_shift, s_wkv = run_group(x_sample.reshape(nsb, d), True)
    return (y_p, y_s, p_kv, p_kr, p_conv, p_shift, p_wkv, s_kv, s_kr, s_conv, s_shift, s_wkv)
```

```python
import functools

import jax
import jax.numpy as jnp
from jax import lax
from jax.experimental import pallas as pl
from jax.experimental.pallas import tpu as pltpu

F32 = jnp.float32
BF16 = jnp.bfloat16

RMS_EPS = 1e-6
GN_EPS = 64e-5
ROPE_THETA = 10000.0
FFN_RES = 0.5
WKV_CHUNK = 64
WKV_LANES = 1024
ATTN_TQ = 512
V7X_VMEM_LIMIT_BYTES = 56 * 1024 * 1024
HI = lax.Precision.HIGHEST


def _params(sem):
    return pltpu.CompilerParams(dimension_semantics=sem, vmem_limit_bytes=V7X_VMEM_LIMIT_BYTES)


def _dot(a, b, precision=None):
    return jnp.dot(a, b, preferred_element_type=F32, precision=precision)


def _dot_nt(a, b, precision=None):
    return lax.dot_general(a, b, (((1,), (1,)), ((), ())), preferred_element_type=F32, precision=precision)


def _dot_tn(a, b, precision=None):
    return lax.dot_general(a, b, (((0,), (0,)), ((), ())), preferred_element_type=F32, precision=precision)


def _sigmoid(x):
    return 1.0 / (1.0 + jnp.exp(-x))


def _silu(x):
    return x * _sigmoid(x)


def _w(arr, lead=(), roff=0, coff=0):
    return dict(arr=arr, lead=tuple(lead), roff=roff, coff=coff)


def _mm(xs, ws, accs, epi, out_dtypes, *, n, tm, tn, tk=None, kq=1, extras=(), x_pro=None, name):
    m, kdim = xs[0].shape
    tk = kdim if tk is None else tk
    assert m % tm == 0 and kdim % tk == 0, (name, m, tm, kdim, tk)
    nchunk = kdim // tk
    nk = pl.cdiv(nchunk, kq)
    ragged = nk * kq != nchunk
    grid = (m // tm, pl.cdiv(n, tn), nk)
    nx, nw, ne, na, no = len(xs), len(ws), len(extras), len(accs), len(out_dtypes)

    def chunk(k, p):
        c = k * kq + p
        return jnp.minimum(c, nchunk - 1) if ragged else c

    in_specs = []
    for _ in xs:
        for p in range(kq):
            in_specs.append(pl.BlockSpec((tm, tk), lambda i, j, k, p=p: (i, chunk(k, p))))
    for w in ws:
        nlead = len(w["lead"])
        assert w["arr"].ndim == nlead + 2
        for p in range(kq):
            def w_map(i, j, k, w=w, p=p):
                return w["lead"] + (w["roff"] + chunk(k, p), w["coff"] + j)

            in_specs.append(pl.BlockSpec((None,) * nlead + (tk, tn), w_map))
    for (_, bshape, imap) in extras:
        in_specs.append(pl.BlockSpec(bshape, imap))
    out_shapes = [jax.ShapeDtypeStruct((m, n), dt) for dt in out_dtypes]
    out_specs = [pl.BlockSpec((tm, tn), lambda i, j, k: (i, j)) for _ in out_dtypes]

    def kern(*refs):
        x_refs = refs[:nx * kq]
        w_refs = refs[nx * kq:(nx + nw) * kq]
        e_refs = refs[(nx + nw) * kq:(nx + nw) * kq + ne]
        o_refs = refs[(nx + nw) * kq + ne:(nx + nw) * kq + ne + no]
        acc_refs = refs[(nx + nw) * kq + ne + no:]
        k = pl.program_id(2)
        xv = []
        for xi in range(nx):
            pieces = []
            for p in range(kq):
                v = x_refs[xi * kq + p][...]
                if x_pro is not None:
                    v = x_pro(v.astype(F32))
                v = v.astype(BF16)
                if ragged and (nk - 1) * kq + p >= nchunk:
                    v = jnp.where(k * kq + p < nchunk, v, jnp.zeros_like(v))
                pieces.append(v)
            xv.append(pieces[0] if kq == 1 else jnp.concatenate(pieces, axis=1))
        wv = []
        for wi in range(nw):
            pieces = [w_refs[wi * kq + p][...].astype(BF16) for p in range(kq)]
            wv.append(pieces[0] if kq == 1 else jnp.concatenate(pieces, axis=0))
        parts = []
        for acc in accs:
            tot = None
            for (xi, wi) in acc:
                d = _dot(xv[xi], wv[wi])
                tot = d if tot is None else tot + d
            parts.append(tot)

        def finalize(vals):
            ev = []
            for r in e_refs:
                v = r[...]
                ev.append(v.reshape(v.shape[-2:]))
            outs = epi(vals, ev)
            for o_ref, o in zip(o_refs, outs):
                o_ref[...] = o.astype(o_ref.dtype).reshape(o_ref.shape)

        if nk == 1:
            finalize(parts)
        else:
            @pl.when(k == 0)
            def _():
                for a_ref, p in zip(acc_refs, parts):
                    a_ref[...] = p

            @pl.when(k > 0)
            def _():
                for a_ref, p in zip(acc_refs, parts):
                    a_ref[...] += p

            @pl.when(k == nk - 1)
            def _():
                finalize([a_ref[...] for a_ref in acc_refs])

    scratch = [pltpu.VMEM((tm, tn), F32) for _ in range(na)] if nk > 1 else []
    args = []
    for x in xs:
        args += [x] * kq
    for w in ws:
        args += [w["arr"]] * kq
    args += [e[0] for e in extras]
    outs = pl.pallas_call(
        kern, out_shape=out_shapes, grid=grid, in_specs=in_specs, out_specs=out_specs,
        scratch_shapes=scratch, name=name,
        compiler_params=_params(("parallel", "parallel", "arbitrary")),
    )(*args)
    return outs


def _ex_row(arr, tm, tn, coff=0):
    return (arr, (tm, tn), lambda i, j, k: (i, coff + j))


def _ex_col(arr, tn, coff=0):
    return (arr, (1, tn), lambda i, j, k: (0, coff + j))


class _Mod:
    def __init__(self, ada, d, per_row, row0, t_len):
        self.d, self.per_row, self.row0, self.t_len = d, per_row, row0, t_len
        rows = ada.shape[0]
        self.a2 = ada
        self.a3 = ada.reshape(rows, 1, ada.shape[1])
        self.a3r = ada.reshape(1, rows, ada.shape[1])

    def norm_operand(self, sub, kind, tt):
        slab = sub * 3 + kind
        if self.per_row:
            return self.a3r, (1, tt, self.d), (lambda b, t: (0, t, slab))
        row0 = self.row0
        return self.a3, (1, 1, self.d), (lambda b, t: (row0 + b, 0, slab))

    def mm_extra(self, sub, kind, tm, tn):
        cb = (sub * 3 + kind) * (self.d // tn)
        if self.per_row:
            return (self.a2, (tm, tn), lambda i, j, k: (i, cb + j))
        row0, t_len = self.row0, self.t_len
        return (self.a3, (1, 1, tn), lambda i, j, k: (row0 + (i * tm) // t_len, 0, cb + j))


def _adanorm(x3, g, mod, sub, out_dtype, tt, name):
    b, t, d = x3.shape
    use_mod = mod is not None

    def kern(*refs):
        if use_mod:
            x_ref, g_ref, sh_ref, sc_ref, o_ref = refs
        else:
            x_ref, g_ref, o_ref = refs
        xf = x_ref[0]
        y = xf * lax.rsqrt(jnp.mean(xf * xf, axis=-1, keepdims=True) + RMS_EPS) * g_ref[...]
        if use_mod:
            y = y * (1.0 + sc_ref[0]) + sh_ref[0]
        o_ref[0] = y.astype(o_ref.dtype)

    in_specs = [pl.BlockSpec((1, tt, d), lambda bi, ti: (bi, ti, 0)),
                pl.BlockSpec((1, d), lambda bi, ti: (0, 0))]
    args = [x3, g.reshape(1, d)]
    if use_mod:
        for kind in (0, 1):
            arr, bshape, imap = mod.norm_operand(sub, kind, tt)
            in_specs.append(pl.BlockSpec(bshape, imap))
            args.append(arr)
    return pl.pallas_call(
        kern, out_shape=jax.ShapeDtypeStruct((b, t, d), out_dtype), grid=(b, t // tt),
        in_specs=in_specs, out_specs=pl.BlockSpec((1, tt, d), lambda bi, ti: (bi, ti, 0)),
        name=name, compiler_params=_params(("parallel", "parallel")),
    )(*args)


def _rope_tables(pos, half):
    inv_freq = ROPE_THETA ** (-jnp.arange(half, dtype=F32) / half)
    ang = pos.astype(F32)[:, None] * inv_freq[None, :]
    cos, sin = jnp.cos(ang), jnp.sin(ang)
    return jnp.concatenate([cos, cos, -sin, sin], axis=-1)


def _rope_cols(w, half):
    return jnp.concatenate([w, w[..., half:], w[..., :half]], axis=-1)


def _kvprep(a, kr_raw, tab, q_norm_g, kv_norm_g, ql, kvl, rd, tm, name):
    m = a.shape[0]

    def kern(a_ref, kr_ref, tab_ref, qg_ref, kg_ref, qn_ref, ckv_ref, kro_ref, kcat_ref):
        av = a_ref[...]
        cq, ck = av[:, :ql], av[:, ql:]
        qn = cq * lax.rsqrt(jnp.mean(cq * cq, axis=-1, keepdims=True) + RMS_EPS) * qg_ref[...]
        qn_ref[...] = qn.astype(BF16)
        ckv = ck * lax.rsqrt(jnp.mean(ck * ck, axis=-1, keepdims=True) + RMS_EPS) * kg_ref[...]
        ckv_ref[...] = ckv
        r = kr_ref[...] * tab_ref[...]
        rope = r + pltpu.roll(r, rd, 1)
        kro_ref[...] = rope[:, :rd]
        lane = lax.broadcasted_iota(jnp.int32, rope.shape, 1)
        kcat_ref[...] = jnp.concatenate([ckv, jnp.where(lane < rd, rope, 0.0)], axis=1).astype(BF16)

    return pl.pallas_call(
        kern,
        out_shape=[jax.ShapeDtypeStruct((m, ql), BF16), jax.ShapeDtypeStruct((m, kvl), F32),
                   jax.ShapeDtypeStruct((m, rd), F32), jax.ShapeDtypeStruct((m, kvl + 2 * rd), BF16)],
        grid=(m // tm,),
        in_specs=[pl.BlockSpec((tm, ql + kvl), lambda i: (i, 0)),
                  pl.BlockSpec((tm, 2 * rd), lambda i: (i, 0)),
                  pl.BlockSpec((tm, 2 * rd), lambda i: (i, 0)),
                  pl.BlockSpec((1, ql), lambda i: (0, 0)),
                  pl.BlockSpec((1, kvl), lambda i: (0, 0))],
        out_specs=[pl.BlockSpec((tm, ql), lambda i: (i, 0)), pl.BlockSpec((tm, kvl), lambda i: (i, 0)),
                   pl.BlockSpec((tm, rd), lambda i: (i, 0)), pl.BlockSpec((tm, kvl + 2 * rd), lambda i: (i, 0))],
        name=name, compiler_params=_params(("parallel",)),
    )(a, kr_raw, tab, q_norm_g.reshape(1, ql), kv_norm_g.reshape(1, kvl))


def _qproj(qn, w_nope, w_rope, w_uk_t, tab, tm, name):
    m, ql = qn.shape
    nh, _, nd = w_nope.shape
    kvl = w_uk_t.shape[2]
    rd2 = w_rope.shape[2]

    def kern(qn_ref, wn_ref, wr_ref, wuk_ref, tab_ref, o_ref):
        q = qn_ref[...]
        q_nope = _dot(q, wn_ref[...].astype(BF16))
        q_lat = _dot(q_nope.astype(BF16), wuk_ref[...].astype(BF16))
        r = _dot(q, wr_ref[...].astype(BF16)) * tab_ref[...]
        q_rope = r + pltpu.roll(r, rd2 // 2, 1)
        o_ref[...] = jnp.concatenate([q_lat, q_rope], axis=1).astype(BF16)

    return pl.pallas_call(
        kern, out_shape=jax.ShapeDtypeStruct((nh, m, kvl + rd2), BF16), grid=(nh, m // tm),
        in_specs=[pl.BlockSpec((tm, ql), lambda h, i: (i, 0)),
                  pl.BlockSpec((None, ql, nd), lambda h, i: (h, 0, 0)),
                  pl.BlockSpec((None, ql, rd2), lambda h, i: (h, 0, 0)),
                  pl.BlockSpec((None, nd, kvl), lambda h, i: (h, 0, 0)),
                  pl.BlockSpec((tm, rd2), lambda h, i: (i, 0))],
        out_specs=pl.BlockSpec((None, tm, kvl + rd2), lambda h, i: (h, i, 0)),
        name=name, compiler_params=_params(("parallel", "parallel")),
    )(qn, w_nope, w_rope, w_uk_t, tab)


def _prompt_attention(qcat, kcat, n_seq, t_len, kvl, scale, tq, name):
    nh, m, e = qcat.shape
    nq = t_len // tq

    def kern(q_ref, k_ref, o_ref, m_ref, l_ref, acc_ref):
        qi = pl.program_id(2)
        q = q_ref[...]
        m_ref[...] = jnp.full(m_ref.shape, -jnp.inf, F32)
        l_ref[...] = jnp.zeros(l_ref.shape, F32)
        acc_ref[...] = jnp.zeros(acc_ref.shape, F32)

        def update(j, masked):
            kb = k_ref[pl.ds(pl.multiple_of(j * tq, tq), tq), :]
            s = _dot_nt(q, kb) * scale
            if masked:
                row = lax.broadcasted_iota(jnp.int32, s.shape, 0)
                col = lax.broadcasted_iota(jnp.int32, s.shape, 1)
                s = jnp.where(col <= row, s, -jnp.inf)
            m_prev = m_ref[...]
            m_new = jnp.maximum(m_prev, jnp.max(s, axis=-1, keepdims=True))
            alpha = jnp.exp(m_prev - m_new)
            p = jnp.exp(s - m_new)
            l_ref[...] = l_ref[...] * alpha + jnp.sum(p, axis=-1, keepdims=True)
            acc_ref[...] = acc_ref[...] * alpha + _dot(p.astype(BF16), kb[:, :kvl])
            m_ref[...] = m_new

        def body(j, carry):
            update(j, False)
            return carry

        lax.fori_loop(0, qi, body, 0)
        update(qi, True)
        o_ref[...] = (acc_ref[...] / l_ref[...]).astype(o_ref.dtype)

    return pl.pallas_call(
        kern, out_shape=jax.ShapeDtypeStruct((nh, m, kvl), BF16), grid=(n_seq, nh, nq),
        in_specs=[pl.BlockSpec((None, tq, e), lambda b, h, qi: (h, b * nq + qi, 0)),
                  pl.BlockSpec((t_len, e), lambda b, h, qi: (b, 0))],
        out_specs=pl.BlockSpec((None, tq, kvl), lambda b, h, qi: (h, b * nq + qi, 0)),
        scratch_shapes=[pltpu.VMEM((tq, 1), F32), pltpu.VMEM((tq, 1), F32), pltpu.VMEM((tq, kvl), F32)],
        name=name, compiler_params=_params(("parallel", "parallel", "arbitrary")),
    )(qcat, kcat)


def _decode_attention(q, cache_kv, cache_kr_t, layer, page_table, ckv_new, kr_new, scale, pp, name):
    bsz, nh, e = q.shape
    page, kvl = cache_kv.shape[2], cache_kv.shape[3]
    rd = cache_kr_t.shape[2]
    n_pages = page_table.shape[1]
    assert n_pages % pp == 0
    ns = n_pages // pp

    def kern(pt_ref, q_ref, *refs):
        kv_refs = refs[:pp]
        kr_refs = refs[pp:2 * pp]
        cn_ref, rn_ref, o_ref, m_ref, l_ref, acc_ref, kvb_ref, krb_ref = refs[2 * pp:]
        s_idx = pl.program_id(1)

        @pl.when(s_idx == 0)
        def _():
            m_ref[...] = jnp.full(m_ref.shape, -jnp.inf, F32)
            l_ref[...] = jnp.zeros(l_ref.shape, F32)
            acc_ref[...] = jnp.zeros(acc_ref.shape, F32)

        qv = q_ref[0]
        ql, qr = qv[:, :kvl], qv[:, kvl:kvl + rd]

        def online(s, pv_fn):
            m_prev = m_ref[...]
            m_new = jnp.maximum(m_prev, jnp.max(s, axis=-1, keepdims=True))
            alpha = jnp.exp(m_prev - m_new)
            p = jnp.exp(s - m_new)
            l_ref[...] = l_ref[...] * alpha + jnp.sum(p, axis=-1, keepdims=True)
            acc_ref[...] = acc_ref[...] * alpha + pv_fn(p)
            m_ref[...] = m_new

        for j in range(pp):
            kvb_ref[j * page:(j + 1) * page, :] = kv_refs[j][...].astype(BF16)
            krb_ref[:, j * page:(j + 1) * page] = kr_refs[j][...].astype(BF16)
        kv_all = kvb_ref[...]
        s = (_dot_nt(ql, kv_all) + _dot(qr, krb_ref[...])) * scale
        online(s, lambda p: _dot(p.astype(BF16), kv_all))

        @pl.when(s_idx == ns - 1)
        def _():
            cn = cn_ref[0].astype(BF16).astype(F32)
            rn = rn_ref[0].astype(BF16).astype(F32)
            s_new = (jnp.sum(ql.astype(F32) * cn, axis=-1, keepdims=True)
                     + jnp.sum(qr.astype(F32) * rn, axis=-1, keepdims=True)) * scale
            online(s_new, lambda p: p.astype(BF16).astype(F32) * cn)
            o_ref[0] = acc_ref[...] / l_ref[...]

    in_specs = [pl.BlockSpec((1, nh, e), lambda b, s, pt: (b, 0, 0))]
    for j in range(pp):
        in_specs.append(pl.BlockSpec((None, None, page, kvl),
                                     lambda b, s, pt, j=j: (layer, pt[b, s * pp + j], 0, 0)))
    for j in range(pp):
        in_specs.append(pl.BlockSpec((None, None, rd, page),
                                     lambda b, s, pt, j=j: (layer, pt[b, s * pp + j], 0, 0)))
    in_specs.append(pl.BlockSpec((1, 1, kvl), lambda b, s, pt: (b, 0, 0)))
    in_specs.append(pl.BlockSpec((1, 1, rd), lambda b, s, pt: (b, 0, 0)))
    grid_spec = pltpu.PrefetchScalarGridSpec(
        num_scalar_prefetch=1, grid=(bsz, ns), in_specs=in_specs,
        out_specs=pl.BlockSpec((1, nh, kvl), lambda b, s, pt: (b, 0, 0)),
        scratch_shapes=[pltpu.VMEM((nh, 1), F32), pltpu.VMEM((nh, 1), F32), pltpu.VMEM((nh, kvl), F32),
                        pltpu.VMEM((pp * page, kvl), BF16), pltpu.VMEM((rd, pp * page), BF16)])
    return pl.pallas_call(
        kern, out_shape=jax.ShapeDtypeStruct((bsz, nh, kvl), F32), grid_spec=grid_spec, name=name,
        compiler_params=_params(("parallel", "arbitrary")),
    )(page_table, q, *([cache_kv] * pp), *([cache_kr_t] * pp), ckv_new, kr_new)


def _oproj(o_lat, w_uv, tm, name):
    nh, m, kvl = o_lat.shape
    vd = w_uv.shape[2]

    def kern(o_ref, w_ref, y_ref):
        y_ref[...] = _dot(o_ref[...].astype(BF16), w_ref[...].astype(BF16)).astype(BF16)

    return pl.pallas_call(
        kern, out_shape=jax.ShapeDtypeStruct((m, nh * vd), BF16), grid=(nh, m // tm),
        in_specs=[pl.BlockSpec((None, tm, kvl), lambda h, i: (h, i, 0)),
                  pl.BlockSpec((None, kvl, vd), lambda h, i: (h, 0, 0))],
        out_specs=pl.BlockSpec((tm, vd), lambda h, i: (i, h)),
        name=name, compiler_params=_params(("parallel", "parallel")),
    )(o_lat, w_uv)


def _conv_prompt(bcu, conv_w, n_seq, t_len, cd, tc, name):
    m = bcu.shape[0]
    ncb = cd // tc
    cw = conv_w.shape[0]

    def kern(b_ref, c_ref, u_ref, w_ref, y_ref, st_ref):
        xg = c_ref[...] * u_ref[...]
        w = w_ref[...]
        row = lax.broadcasted_iota(jnp.int32, xg.shape, 0)
        y = xg * w[cw - 1:cw]
        for d in range(1, cw):
            sh = jnp.where(row >= d, pltpu.roll(xg, d, 0), 0.0)
            y = y + sh * w[cw - 1 - d:cw - d]
        y_ref[...] = (b_ref[...] * y).astype(BF16)
        st_ref[0] = xg[t_len - (cw - 1):, :]

    return pl.pallas_call(
        kern, out_shape=[jax.ShapeDtypeStruct((m, cd), BF16), jax.ShapeDtypeStruct((n_seq, cw - 1, cd), F32)],
        grid=(n_seq, ncb),
        in_specs=[pl.BlockSpec((t_len, tc), lambda b, j: (b, j)),
                  pl.BlockSpec((t_len, tc), lambda b, j: (b, ncb + j)),
                  pl.BlockSpec((t_len, tc), lambda b, j: (b, 2 * ncb + j)),
                  pl.BlockSpec((cw, tc), lambda b, j: (0, j))],
        out_specs=[pl.BlockSpec((t_len, tc), lambda b, j: (b, j)),
                   pl.BlockSpec((1, cw - 1, tc), lambda b, j: (b, 0, j))],
        name=name, compiler_params=_params(("parallel", "parallel")),
    )(bcu, bcu, bcu, conv_w)


def _conv_step(bcu, prev, conv_w, cd, tc, name):
    bsz = bcu.shape[0]
    ncb = cd // tc
    cw = conv_w.shape[0]

    def kern(*refs):
        b_ref, c_ref, u_ref, w_ref = refs[:4]
        p_refs = refs[4:4 + cw - 1]
        y_ref = refs[4 + cw - 1]
        st_refs = refs[5 + cw - 1:]
        xg = c_ref[...] * u_ref[...]
        w = w_ref[...]
        y = xg * w[cw - 1:cw]
        for d in range(cw - 1):
            y = y + p_refs[d][...] * w[d:d + 1]
        y_ref[...] = (b_ref[...] * y).astype(BF16)
        for d in range(cw - 2):
            st_refs[d][...] = p_refs[d + 1][...]
        st_refs[cw - 2][...] = xg

    in_specs = [pl.BlockSpec((bsz, tc), lambda j: (0, j)),
                pl.BlockSpec((bsz, tc), lambda j: (0, ncb + j)),
                pl.BlockSpec((bsz, tc), lambda j: (0, 2 * ncb + j)),
                pl.BlockSpec((cw, tc), lambda j: (0, j))]
    for d in range(cw - 1):
        in_specs.append(pl.BlockSpec((bsz, tc), lambda j, d=d: (0, d * ncb + j)))
    outs = pl.pallas_call(
        kern,
        out_shape=[jax.ShapeDtypeStruct((bsz, cd), BF16)] + [jax.ShapeDtypeStruct((bsz, cd), F32)] * (cw - 1),
        grid=(ncb,), in_specs=in_specs,
        out_specs=[pl.BlockSpec((bsz, tc), lambda j: (0, j))] * cw,
        name=name, compiler_params=_params(("parallel",)),
    )(bcu, bcu, bcu, conv_w, *([prev] * (cw - 1)))
    return outs[0], jnp.stack(outs[1:], axis=1)


def _token_mix(h3, prev3, mu, tc, name):
    b, t, d = h3.shape
    nmix = mu.shape[0]
    shift_here = prev3 is None

    def kern(*refs):
        if shift_here:
            h_ref, mu_ref = refs[:2]
            o_refs = refs[2:]
        else:
            h_ref, p_ref, mu_ref = refs[:3]
            o_refs = refs[3:]
        hh = h_ref[0]
        if shift_here:
            row = lax.broadcasted_iota(jnp.int32, hh.shape, 0)
            prev = jnp.where(row >= 1, pltpu.roll(hh, 1, 0), 0.0)
        else:
            prev = p_ref[0]
        xx = prev - hh
        mv = mu_ref[...]
        for i in range(nmix):
            o_refs[i][0] = (hh + xx * mv[i:i + 1]).astype(BF16)

    blk = pl.BlockSpec((1, t, tc), lambda bi, j: (bi, 0, j))
    in_specs = [blk] + ([] if shift_here else [blk]) + [pl.BlockSpec((nmix, tc), lambda bi, j: (0, j))]
    args = [h3] + ([] if shift_here else [prev3]) + [mu]
    return pl.pallas_call(
        kern, out_shape=[jax.ShapeDtypeStruct((b, t, d), BF16)] * nmix, grid=(b, d // tc),
        in_specs=in_specs, out_specs=[blk] * nmix,
        name=name, compiler_params=_params(("parallel", "parallel")),
    )(*args)


def _wkv_head_prep(r, k, a, kkw, kaw):
    kk = k * kkw
    kk = kk * lax.rsqrt(jnp.maximum(jnp.sum(kk * kk, axis=-1, keepdims=True), 1e-24))
    kmod = k * (1.0 + (a - 1.0) * kaw)
    return kk, kmod


def _wkv_head_post(y, r, kmod, v, g, rkw, lnw, lnb):
    mean = jnp.mean(y, axis=-1, keepdims=True)
    var = jnp.mean(jnp.square(y - mean), axis=-1, keepdims=True)
    yn = (y - mean) * lax.rsqrt(var + GN_EPS) * lnw + lnb
    bonus = jnp.sum(r * kmod * rkw, axis=-1, keepdims=True) * v
    return (yn + bonus) * g


def _wkv_prompt(r, k, v, a, lw, g, k_k, k_a, r_k, ln_w, ln_b, n_seq, t_len, hs, name):
    m, d = r.shape
    c = WKV_CHUNK
    hl = min(WKV_LANES, d)
    hpb = hl // hs
    nh = d // hs
    nc = t_len // c

    def kern(r_ref, k_ref, v_ref, a_ref, lw_ref, g_ref, kk_ref, ka_ref, rk_ref, lnw_ref, lnb_ref,
             o_ref, s_ref):
        ci = pl.program_id(2)

        @pl.when(ci == 0)
        def _():
            s_ref[...] = jnp.zeros(s_ref.shape, F32)

        ri = lax.broadcasted_iota(jnp.int32, (c, c), 0)
        cj = lax.broadcasted_iota(jnp.int32, (c, c), 1)
        strict = cj < ri
        ri2 = lax.broadcasted_iota(jnp.int32, (c, 2 * c), 0)
        cj2 = lax.broadcasted_iota(jnp.int32, (c, 2 * c), 1)
        incl2 = jnp.where(cj2 >= c, cj2 - c, cj2) <= ri2
        tri = jnp.where(cj <= ri, 1.0, 0.0).astype(BF16)
        eye = jnp.where(cj == ri, 1.0, 0.0).astype(F32)
        heads = range(hpb)
        sls = [slice(hh * hs, (hh + 1) * hs) for hh in heads]

        lw_all = lw_ref[...]
        lw_hi = lw_all.astype(BF16)
        lw_lo = (lw_all - lw_hi.astype(F32)).astype(BF16)
        cum_all = _dot(tri, lw_hi) + _dot(tri, lw_lo)
        at, rt, bk_t, bk_h, vb, kmods, gtot = [], [], [], [], [], [], []
        for hh in heads:
            sl = sls[hh]
            rr, k0, aa, lwh = r_ref[:, sl], k_ref[:, sl], a_ref[:, sl], lw_all[:, sl]
            kk, kmod = _wkv_head_prep(rr, k0, aa, kk_ref[:, sl], ka_ref[:, sl])
            avec, bvec = -kk, kk * aa
            cum = cum_all[:, sl]
            tot = cum[c - 1:c, :]
            at.append((avec * jnp.exp(cum - lwh)).astype(BF16))
            rt.append((rr * jnp.exp(cum)).astype(BF16))
            e_neg = jnp.exp(-cum)
            e_tot = jnp.exp(tot - cum)
            bk_t.append(jnp.concatenate([bvec * e_neg, kmod * e_neg], axis=0).astype(BF16))
            bk_h.append(jnp.concatenate([bvec * e_tot, kmod * e_tot], axis=0).astype(BF16))
            vb.append(v_ref[:, sl].astype(BF16))
            kmods.append(kmod)
            gtot.append(jnp.exp(tot))
        a_ab = [jnp.where(strict, _dot_nt(at[hh], bk_t[hh][:c]), 0.0) for hh in heads]
        a_ak = [jnp.where(strict, _dot_nt(at[hh], bk_t[hh][c:]), 0.0).astype(BF16) for hh in heads]
        a_r = [jnp.where(incl2, _dot_nt(rt[hh], bk_t[hh]), 0.0).astype(BF16) for hh in heads]
        s0 = [s_ref[0, hh] for hh in heads]
        ws = [_dot_nt(jnp.concatenate([at[hh], rt[hh]], axis=0), s0[hh].astype(BF16)) for hh in heads]
        akv = [_dot(a_ak[hh], vb[hh]) for hh in heads]
        tinv = [eye + a_ab[hh] for hh in heads]
        lp = [a_ab[hh].astype(BF16) for hh in heads]
        n_done = 2
        while n_done < c:
            lp = [_dot(lp[hh], lp[hh]).astype(BF16) for hh in heads]
            tinv = [tinv[hh] + _dot(tinv[hh].astype(BF16), lp[hh]) for hh in heads]
            n_done *= 2
        p = [_dot(tinv[hh].astype(BF16), (ws[hh][:c] + akv[hh]).astype(BF16)) for hh in heads]
        z = [jnp.concatenate([p[hh].astype(BF16), vb[hh]], axis=0) for hh in heads]
        y = [ws[hh][c:] + _dot(a_r[hh], z[hh]) for hh in heads]
        s1 = [s0[hh] * gtot[hh] + _dot_tn(z[hh], bk_h[hh]) for hh in heads]
        for hh in heads:
            sl = sls[hh]
            s_ref[0, hh] = s1[hh]
            o_ref[:, sl] = _wkv_head_post(y[hh], r_ref[:, sl], kmods[hh], v_ref[:, sl], g_ref[:, sl],
                                          rk_ref[:, sl], lnw_ref[:, sl], lnb_ref[:, sl]).astype(BF16)

    blk = pl.BlockSpec((c, hl), lambda b, hg, ci: (b * nc + ci, hg))
    wblk = pl.BlockSpec((1, hl), lambda b, hg, ci: (0, hg))
    return pl.pallas_call(
        kern, out_shape=[jax.ShapeDtypeStruct((m, d), BF16), jax.ShapeDtypeStruct((n_seq, nh, hs, hs), F32)],
        grid=(n_seq, d // hl, nc),
        in_specs=[blk] * 6 + [wblk] * 5,
        out_specs=[blk, pl.BlockSpec((1, hpb, hs, hs), lambda b, hg, ci: (b, hg, 0, 0))],
        name=name, compiler_params=_params(("parallel", "parallel", "arbitrary")),
    )(r, k, v, a, lw, g, k_k, k_a, r_k, ln_w, ln_b)


def _wkv_step(r, k, v, a, lw, g, s0, k_k, k_a, r_k, ln_w, ln_b, bb, hb, name):
    bsz, nh, _, hs = r.shape

    def kern(r_ref, k_ref, v_ref, a_ref, lw_ref, g_ref, s_ref, kk_ref, ka_ref, rk_ref, lnw_ref, lnb_ref,
             o_ref, so_ref):
        rr, k0, vv, aa, gg = r_ref[...], k_ref[...], v_ref[...], a_ref[...], g_ref[...]
        dec = jnp.exp(lw_ref[...])
        kk, kmod = _wkv_head_prep(rr, k0, aa, kk_ref[...], ka_ref[...])
        avec, bvec = -kk, kk * aa
        s = s_ref[...]
        ri = lax.broadcasted_iota(jnp.int32, (hs, hs), 0)
        cj = lax.broadcasted_iota(jnp.int32, (hs, hs), 1)
        eye = jnp.where(ri == cj, 1.0, 0.0).astype(F32)
        sa = jnp.sum(s * avec, axis=-1, keepdims=True)
        vcol = jnp.sum(eye * vv, axis=-1, keepdims=True)
        s1 = s * dec + sa * bvec + vcol * kmod
        so_ref[...] = s1
        ycol = jnp.sum(s1 * rr, axis=-1, keepdims=True)
        y = jnp.sum(eye * ycol, axis=-2, keepdims=True)
        o_ref[...] = _wkv_head_post(y, rr, kmod, vv, gg, rk_ref[...], lnw_ref[...], lnb_ref[...])

    vblk = pl.BlockSpec((bb, hb, 1, hs), lambda i, j: (i, j, 0, 0))
    sblk = pl.BlockSpec((bb, hb, hs, hs), lambda i, j: (i, j, 0, 0))
    wblk = pl.BlockSpec((1, hb, 1, hs), lambda i, j: (0, j, 0, 0))
    return pl.pallas_call(
        kern, out_shape=[jax.ShapeDtypeStruct((bsz, nh, 1, hs), F32), jax.ShapeDtypeStruct((bsz, nh, hs, hs), F32)],
        grid=(bsz // bb, nh // hb),
        in_specs=[vblk] * 6 + [sblk] + [wblk] * 5,
        out_specs=[vblk, sblk],
        name=name, compiler_params=_params(("parallel", "parallel")),
    )(r, k, v, a, lw, g, s0, k_k, k_a, r_k, ln_w, ln_b)


def _pick(n, candidates):
    for c in candidates:
        if n % c == 0:
            return c
    return n


def kernel(x_prompt, x_sample, cache_kv_latent, cache_k_rope, state_conv, state_rwkv_shift, state_rwkv_wkv, page_table, c_prompt, c_sample, w_ada, b_ada, norm_g, ffn_w1, ffn_w3, ffn_w2, mla_w_in, mla_q_norm, mla_kv_norm, mla_w_uq, mla_w_ukv, conv_w, mix_w_out, rwkv_mu, rwkv_w0, rwkv_w1, rwkv_w2, rwkv_a0, rwkv_a1, rwkv_a2, rwkv_g1, rwkv_g2, rwkv_k_k, rwkv_k_a, rwkv_r_k, rwkv_w_r, rwkv_w_k, rwkv_w_v, rwkv_w_o, rwkv_ln_w, rwkv_ln_b, final_norm_g):
    nb, seq, d = x_prompt.shape
    nsb, dec_seq, _ = x_sample.shape
    assert dec_seq == 1
    depth = w_ada.shape[0]
    dff = ffn_w1.shape[-1]
    ql = mla_q_norm.shape[1]
    kvl = mla_kv_norm.shape[1]
    rd = cache_k_rope.shape[-1]
    nh_mla = mla_w_ukv.shape[2]
    nd = mla_w_uq.shape[2] // nh_mla - rd
    vd = mla_w_ukv.shape[3] - nd
    cd = conv_w.shape[-1]
    cw = conv_w.shape[1]
    hs = state_rwkv_wkv.shape[-1]
    nh_rwkv = d // hs
    past_len = page_table.shape[1] * cache_kv_latent.shape[2]
    mla_scale = float((nd + rd) ** -0.5)
    mp = nb * seq

    n_c = nsb + nb
    n_c_pad = -(-n_c // 16) * 16
    c_all = jnp.concatenate([c_sample, c_prompt, jnp.zeros((n_c_pad - n_c, d), F32)], axis=0)
    t_ada = _pick(9 * d, (512, 256, 128))
    adas = []
    for layer in range(depth):
        (ada,) = _mm([c_all], [_w(w_ada, (layer,))], [[(0, 0)]],
                     lambda acc, ev: (acc[0] + ev[0],), [F32],
                     n=9 * d, tm=n_c_pad, tn=t_ada, x_pro=_silu,
                     extras=[_ex_col(b_ada[layer].reshape(1, 9 * d), t_ada)], name=f"ada_l{layer}")
        adas.append(ada)

    def run_group(x2, is_sample):
        m = x2.shape[0]
        if is_sample:
            n_seq, t_len = m, 1
            xshape = (1, m, d)
            tm = m
            tt = m
            pos = jnp.full((m,), past_len, jnp.int32)
        else:
            n_seq, t_len = nb, seq
            xshape = (nb, seq, d)
            tm = _pick(seq, (1024, 512, 256, 128))
            tt = _pick(seq, (256, 128))
            pos = jnp.tile(jnp.arange(seq, dtype=jnp.int32), nb)
        tn = _pick(d, (512, 256, 128))
        tab = _rope_tables(pos, rd // 2)
        kv_rows, kr_rows, conv_states, shift_states, wkv_states = [], [], [], [], []

        def ffn(x2, mod, layer, idx, sub, tag):
            h = _adanorm(x2.reshape(xshape), norm_g[layer, sub], mod, sub, BF16, tt, f"norm_{tag}").reshape(m, d)
            tnf = _pick(dff, (256, 128))
            (gact,) = _mm([h], [_w(ffn_w1, (layer, idx)), _w(ffn_w3, (layer, idx))], [[(0, 0)], [(0, 1)]],
                          lambda acc, ev: (_silu(acc[0]) * acc[1],), [BF16],
                          n=dff, tm=tm, tn=tnf, name=f"ffn_up_{tag}")
            tkd = _pick(dff, (256, 128)) if not is_sample else dff
            tnd = _pick(d, (1024, 512, 256, 128)) if not is_sample else _pick(d, (256, 128))
            (xo,) = _mm([gact], [_w(ffn_w2, (layer, idx))], [[(0, 0)]],
                        lambda acc, ev: (ev[0] + FFN_RES * ev[1] * acc[0],), [F32],
                        n=d, tm=tm, tn=tnd, tk=tkd, kq=1 if is_sample else 4,
                        extras=[_ex_row(x2, tm, tnd), mod.mm_extra(sub, 2, tm, tnd)], name=f"ffn_down_{tag}")
            return xo

        for layer in range(depth):
            tag = f"{'s' if is_sample else 'p'}{layer}"
            mod = _Mod(adas[layer], d, is_sample, nsb, t_len)
            x2 = ffn(x2, mod, layer, 0, 0, tag + "a")
            if layer % 2 == 0:
                i = layer // 2
                h = _adanorm(x2.reshape(xshape), norm_g[layer, 1], mod, 1, BF16, tt, f"norm_mix_{tag}").reshape(m, d)
                w_in = mla_w_in[i]
                n_a = ql + kvl
                (a_proj,) = _mm([h], [_w(w_in)], [[(0, 0)]], lambda acc, ev: (acc[0],), [F32],
                                n=n_a, tm=tm, tn=_pick(n_a, (512, 256, 128)), name=f"mla_in_a_{tag}")
                w_kr = _rope_cols(w_in[:, n_a:n_a + rd], rd // 2)
                (kr_raw,) = _mm([h], [_w(w_kr)], [[(0, 0)]], lambda acc, ev: (acc[0],), [F32],
                                n=2 * rd, tm=tm, tn=2 * rd, name=f"mla_in_kr_{tag}")
                w_bcu = w_in[:, n_a + rd:]
                (bcu,) = _mm([h], [_w(w_bcu)], [[(0, 0)]], lambda acc, ev: (acc[0],), [F32],
                             n=3 * cd, tm=tm, tn=_pick(cd, (512, 256, 128)), name=f"mla_in_bcu_{tag}")
                qn, ckv, kr, kcat = _kvprep(a_proj, kr_raw, tab, mla_q_norm[i], mla_kv_norm[i], ql, kvl, rd,
                                            _pick(m, (256, 128)), f"kvprep_{tag}")
                w_uq = mla_w_uq[i].reshape(ql, nh_mla, nd + rd)
                w_nope = jnp.transpose(w_uq[:, :, :nd], (1, 0, 2))
                w_rope = _rope_cols(jnp.transpose(w_uq[:, :, nd:], (1, 0, 2)), rd // 2)
                w_uk_t = jnp.transpose(mla_w_ukv[i][:, :, :nd], (1, 2, 0))
                w_uv = jnp.transpose(mla_w_ukv[i][:, :, nd:], (1, 0, 2))
                qcat = _qproj(qn, w_nope, w_rope, w_uk_t, tab, _pick(m, (512, 256, 128)), f"qproj_{tag}")
                if is_sample:
                    o_lat = _decode_attention(jnp.transpose(qcat, (1, 0, 2)), cache_kv_latent,
                                              jnp.swapaxes(cache_k_rope, 2, 3), i,
                                              page_table, ckv.reshape(m, 1, kvl), kr.reshape(m, 1, rd),
                                              mla_scale, _pick(page_table.shape[1], (16, 8, 4, 2, 1)), f"attn_{tag}")
                    o_lat = jnp.transpose(o_lat, (1, 0, 2))
                    prev = state_conv[i].reshape(m, (cw - 1) * cd)
                    conv_y, conv_new = _conv_step(bcu, prev, conv_w[i], cd, _pick(cd, (512, 256, 128)), f"conv_{tag}")
                else:
                    o_lat = _prompt_attention(qcat, kcat, n_seq, t_len, kvl, mla_scale,
                                              _pick(t_len, (ATTN_TQ, 128)), f"attn_{tag}")
                    conv_y, conv_new = _conv_prompt(bcu, conv_w[i], n_seq, t_len, cd, _pick(cd, (256, 128)),
                                                    f"conv_{tag}")
                o_att = _oproj(o_lat, w_uv, _pick(m, (512, 256, 128)), f"oproj_{tag}")
                assert nh_mla * vd == cd
                (x2,) = _mm([o_att, conv_y], [_w(mix_w_out, (i,)), _w(mix_w_out, (i,), roff=1)], [[(0, 0), (1, 1)]],
                            lambda acc, ev: (ev[0] + ev[1] * acc[0],), [F32],
                            n=d, tm=tm, tn=tn, extras=[_ex_row(x2, tm, tn), mod.mm_extra(1, 2, tm, tn)],
                            name=f"mix_out_{tag}")
                kv_rows.append(ckv.reshape(n_seq, t_len, kvl))
                kr_rows.append(kr.reshape(n_seq, t_len, rd))
                conv_states.append(conv_new)
            else:
                j = layer // 2
                h3 = _adanorm(x2.reshape(xshape), norm_g[layer, 1], mod, 1, F32, tt, f"norm_mix_{tag}")
                tcm = _pick(d, (256, 128))
                if is_sample:
                    mixes = _token_mix(h3, state_rwkv_shift[j].reshape(xshape), rwkv_mu[j], tcm, f"tmix_{tag}")
                    shift_states.append(h3.reshape(m, d))
                else:
                    mixes = _token_mix(h3, None, rwkv_mu[j], tcm, f"tmix_{tag}")
                    shift_states.append(h3[:, -1])
                xr, xw, xk, xv, xa, xg = (t.reshape(m, d) for t in mixes)

                def proj(x, w, tag2):
                    (o,) = _mm([x], [_w(w, (j,))], [[(0, 0)]], lambda acc, ev: (acc[0],), [F32],
                               n=d, tm=tm, tn=tn, name=f"rwkv_{tag2}_{tag}")
                    return o

                r, k, v = proj(xr, rwkv_w_r, "r"), proj(xk, rwkv_w_k, "k"), proj(xv, rwkv_w_v, "v")
                dl = rwkv_w1.shape[-1]
                (t1,) = _mm([xw], [_w(rwkv_w1, (j,))], [[(0, 0)]], lambda acc, ev: (jnp.tanh(acc[0]),), [BF16],
                            n=dl, tm=tm, tn=dl, name=f"rwkv_w1_{tag}")

                def decay_epi(acc, ev):
                    z = -(ev[0] + acc[0])
                    softplus = jnp.maximum(z, 0.0) + jnp.log(1.0 + jnp.exp(-jnp.abs(z)))
                    return (-jnp.exp(-softplus - 0.5),)

                (lw,) = _mm([t1], [_w(rwkv_w2, (j,))], [[(0, 0)]], decay_epi, [F32],
                            n=d, tm=tm, tn=tn, extras=[_ex_col(rwkv_w0[j].reshape(1, d), tn)], name=f"rwkv_w2_{tag}")
                al = rwkv_a1.shape[-1]
                (t2,) = _mm([xa], [_w(rwkv_a1, (j,))], [[(0, 0)]], lambda acc, ev: (acc[0],), [BF16],
                            n=al, tm=tm, tn=al, name=f"rwkv_a1_{tag}")
                (a,) = _mm([t2], [_w(rwkv_a2, (j,))], [[(0, 0)]], lambda acc, ev: (_sigmoid(ev[0] + acc[0]),), [F32],
                           n=d, tm=tm, tn=tn, extras=[_ex_col(rwkv_a0[j].reshape(1, d), tn)], name=f"rwkv_a2_{tag}")
                gl = rwkv_g1.shape[-1]
                glp = -(-gl // 128) * 128
                g1p = jnp.pad(rwkv_g1[j], ((0, 0), (0, glp - gl)))
                g2p = jnp.pad(rwkv_g2[j], ((0, glp - gl), (0, 0)))
                (t3,) = _mm([xg], [_w(g1p)], [[(0, 0)]], lambda acc, ev: (_sigmoid(acc[0]),), [BF16],
                            n=glp, tm=tm, tn=glp, name=f"rwkv_g1_{tag}")
                (g,) = _mm([t3], [_w(g2p)], [[(0, 0)]], lambda acc, ev: (acc[0],), [F32],
                           n=d, tm=tm, tn=tn, name=f"rwkv_g2_{tag}")
                if is_sample:
                    v4 = lambda t: t.reshape(m, nh_rwkv, 1, hs)
                    w4 = lambda t: t.reshape(1, nh_rwkv, 1, hs)
                    o4, s_new = _wkv_step(v4(r), v4(k), v4(v), v4(a), v4(lw), v4(g), state_rwkv_wkv[j],
                                          w4(rwkv_k_k[j]), w4(rwkv_k_a[j]), w4(rwkv_r_k[j]),
                                          w4(rwkv_ln_w[j]), w4(rwkv_ln_b[j]),
                                          _pick(m, (8, 4, 2, 1)), _pick(nh_rwkv, (16, 8, 4, 2, 1)), f"wkv_{tag}")
                    o_pre = o4.reshape(m, d)
                else:
                    w2d = lambda t: t.reshape(1, d)
                    o_pre, s_new = _wkv_prompt(r, k, v, a, lw, g, w2d(rwkv_k_k[j]), w2d(rwkv_k_a[j]),
                                               w2d(rwkv_r_k[j]), w2d(rwkv_ln_w[j]), w2d(rwkv_ln_b[j]),
                                               n_seq, t_len, hs, f"wkv_{tag}")
                (x2,) = _mm([o_pre], [_w(rwkv_w_o, (j,))], [[(0, 0)]],
                            lambda acc, ev: (ev[0] + ev[1] * acc[0],), [F32],
                            n=d, tm=tm, tn=tn, extras=[_ex_row(x2, tm, tn), mod.mm_extra(1, 2, tm, tn)],
                            name=f"rwkv_o_{tag}")
                wkv_states.append(s_new)
            x2 = ffn(x2, mod, layer, 1, 2, tag + "b")
        y = _adanorm(x2.reshape(xshape), final_norm_g, None, 0, F32, tt, f"final_norm_{'s' if is_sample else 'p'}")
        return (y.reshape(n_seq, t_len, d), jnp.stack(kv_rows), jnp.stack(kr_rows), jnp.stack(conv_states),
                jnp.stack(shift_states), jnp.stack(wkv_states))

    y_p, p_kv, p_kr, p_conv, p_shift, p_wkv = run_group(x_prompt.reshape(mp, d), False)
    y_s, s_kv, s_kr, s_conv, s_shift, s_wkv = run_group(x_sample.reshape(nsb, d), True)
    return (y_p, y_s, p_kv, p_kr, p_conv, p_shift, p_wkv, s_kv, s_kr, s_conv, s_shift, s_wkv)
```

```python
import functools

import jax
import jax.numpy as jnp
from jax import lax
from jax.experimental import pallas as pl
from jax.experimental.pallas import tpu as pltpu

F32 = jnp.float32
BF16 = jnp.bfloat16

RMS_EPS = 1e-6
GN_EPS = 64e-5
ROPE_THETA = 10000.0
FFN_RES = 0.5
WKV_CHUNK = 64
WKV_LANES = 1024
ATTN_TQ = 512
ATTN_HEADS = 4
V7X_VMEM_LIMIT_BYTES = 56 * 1024 * 1024
HI = lax.Precision.HIGHEST


def _params(sem):
    return pltpu.CompilerParams(dimension_semantics=sem, vmem_limit_bytes=V7X_VMEM_LIMIT_BYTES)


def _dot(a, b, precision=None):
    return jnp.dot(a, b, preferred_element_type=F32, precision=precision)


def _dot_nt(a, b, precision=None):
    return lax.dot_general(a, b, (((1,), (1,)), ((), ())), preferred_element_type=F32, precision=precision)


def _dot_tn(a, b, precision=None):
    return lax.dot_general(a, b, (((0,), (0,)), ((), ())), preferred_element_type=F32, precision=precision)


def _sigmoid(x):
    return 1.0 / (1.0 + jnp.exp(-x))


def _silu(x):
    return x * _sigmoid(x)


def _w(arr, lead=(), roff=0, coff=0):
    return dict(arr=arr, lead=tuple(lead), roff=roff, coff=coff)


def _mm(xs, ws, accs, epi, out_dtypes, *, n, tm, tn, tk=None, kq=1, extras=(), x_pro=None, name):
    m, kdim = xs[0].shape
    tk = kdim if tk is None else tk
    assert m % tm == 0 and kdim % tk == 0, (name, m, tm, kdim, tk)
    nchunk = kdim // tk
    nk = pl.cdiv(nchunk, kq)
    ragged = nk * kq != nchunk
    grid = (m // tm, pl.cdiv(n, tn), nk)
    nx, nw, ne, na, no = len(xs), len(ws), len(extras), len(accs), len(out_dtypes)

    def chunk(k, p):
        c = k * kq + p
        return jnp.minimum(c, nchunk - 1) if ragged else c

    in_specs = []
    for _ in xs:
        for p in range(kq):
            in_specs.append(pl.BlockSpec((tm, tk), lambda i, j, k, p=p: (i, chunk(k, p))))
    for w in ws:
        nlead = len(w["lead"])
        assert w["arr"].ndim == nlead + 2
        for p in range(kq):
            def w_map(i, j, k, w=w, p=p):
                return w["lead"] + (w["roff"] + chunk(k, p), w["coff"] + j)

            in_specs.append(pl.BlockSpec((None,) * nlead + (tk, tn), w_map))
    for (_, bshape, imap) in extras:
        in_specs.append(pl.BlockSpec(bshape, imap))
    out_shapes = [jax.ShapeDtypeStruct((m, n), dt) for dt in out_dtypes]
    out_specs = [pl.BlockSpec((tm, tn), lambda i, j, k: (i, j)) for _ in out_dtypes]

    def kern(*refs):
        x_refs = refs[:nx * kq]
        w_refs = refs[nx * kq:(nx + nw) * kq]
        e_refs = refs[(nx + nw) * kq:(nx + nw) * kq + ne]
        o_refs = refs[(nx + nw) * kq + ne:(nx + nw) * kq + ne + no]
        acc_refs = refs[(nx + nw) * kq + ne + no:]
        k = pl.program_id(2)
        xv = []
        for xi in range(nx):
            pieces = []
            for p in range(kq):
                v = x_refs[xi * kq + p][...]
                if x_pro is not None:
                    v = x_pro(v.astype(F32))
                v = v.astype(BF16)
                if ragged and (nk - 1) * kq + p >= nchunk:
                    v = jnp.where(k * kq + p < nchunk, v, jnp.zeros_like(v))
                pieces.append(v)
            xv.append(pieces)
        wv = [[w_refs[wi * kq + p][...].astype(BF16) for p in range(kq)] for wi in range(nw)]

        def accumulate(init):
            outs = []
            for ai, acc in enumerate(accs):
                tot = None if init is None else init[ai]
                for (xi, wi) in acc:
                    for p in range(kq):
                        d = _dot(xv[xi][p], wv[wi][p])
                        tot = d if tot is None else tot + d
                outs.append(tot)
            return outs

        def finalize(vals):
            ev = []
            for r in e_refs:
                v = r[...]
                ev.append(v.reshape(v.shape[-2:]))
            outs = epi(vals, ev)
            for o_ref, o in zip(o_refs, outs):
                o_ref[...] = o.astype(o_ref.dtype).reshape(o_ref.shape)

        if nk == 1:
            finalize(accumulate(None))
        else:
            @pl.when(k == 0)
            def _():
                for a_ref, p in zip(acc_refs, accumulate(None)):
                    a_ref[...] = p

            if nk > 2:
                @pl.when(jnp.logical_and(k > 0, k < nk - 1))
                def _():
                    for a_ref, p in zip(acc_refs, accumulate([a_ref[...] for a_ref in acc_refs])):
                        a_ref[...] = p

            @pl.when(k == nk - 1)
            def _():
                finalize(accumulate([a_ref[...] for a_ref in acc_refs]))

    scratch = [pltpu.VMEM((tm, tn), F32) for _ in range(na)] if nk > 1 else []
    args = []
    for x in xs:
        args += [x] * kq
    for w in ws:
        args += [w["arr"]] * kq
    args += [e[0] for e in extras]
    outs = pl.pallas_call(
        kern, out_shape=out_shapes, grid=grid, in_specs=in_specs, out_specs=out_specs,
        scratch_shapes=scratch, name=name,
        compiler_params=_params(("parallel", "parallel", "arbitrary")),
    )(*args)
    return outs


def _ex_row(arr, tm, tn, coff=0):
    return (arr, (tm, tn), lambda i, j, k: (i, coff + j))


def _ex_col(arr, tn, coff=0):
    return (arr, (1, tn), lambda i, j, k: (0, coff + j))


class _Mod:
    def __init__(self, ada, d, per_row, row0, t_len):
        self.d, self.per_row, self.row0, self.t_len = d, per_row, row0, t_len
        rows = ada.shape[0]
        self.a2 = ada
        self.a3 = ada.reshape(rows, 1, ada.shape[1])
        self.a3r = ada.reshape(1, rows, ada.shape[1])

    def norm_operand(self, sub, kind, tt):
        slab = sub * 3 + kind
        if self.per_row:
            return self.a3r, (1, tt, self.d), (lambda b, t: (0, t, slab))
        row0 = self.row0
        return self.a3, (1, 1, self.d), (lambda b, t: (row0 + b, 0, slab))

    def mm_extra(self, sub, kind, tm, tn):
        cb = (sub * 3 + kind) * (self.d // tn)
        if self.per_row:
            return (self.a2, (tm, tn), lambda i, j, k: (i, cb + j))
        row0, t_len = self.row0, self.t_len
        return (self.a3, (1, 1, tn), lambda i, j, k: (row0 + (i * tm) // t_len, 0, cb + j))


def _adanorm(x3, g, mod, sub, out_dtype, tt, name):
    b, t, d = x3.shape
    use_mod = mod is not None

    def kern(*refs):
        if use_mod:
            x_ref, g_ref, sh_ref, sc_ref, o_ref = refs
        else:
            x_ref, g_ref, o_ref = refs
        xf = x_ref[0]
        y = xf * lax.rsqrt(jnp.mean(xf * xf, axis=-1, keepdims=True) + RMS_EPS) * g_ref[...]
        if use_mod:
            y = y * (1.0 + sc_ref[0]) + sh_ref[0]
        o_ref[0] = y.astype(o_ref.dtype)

    in_specs = [pl.BlockSpec((1, tt, d), lambda bi, ti: (bi, ti, 0)),
                pl.BlockSpec((1, d), lambda bi, ti: (0, 0))]
    args = [x3, g.reshape(1, d)]
    if use_mod:
        for kind in (0, 1):
            arr, bshape, imap = mod.norm_operand(sub, kind, tt)
            in_specs.append(pl.BlockSpec(bshape, imap))
            args.append(arr)
    return pl.pallas_call(
        kern, out_shape=jax.ShapeDtypeStruct((b, t, d), out_dtype), grid=(b, t // tt),
        in_specs=in_specs, out_specs=pl.BlockSpec((1, tt, d), lambda bi, ti: (bi, ti, 0)),
        name=name, compiler_params=_params(("parallel", "parallel")),
    )(*args)


def _rope_tables(pos, half):
    inv_freq = ROPE_THETA ** (-jnp.arange(half, dtype=F32) / half)
    ang = pos.astype(F32)[:, None] * inv_freq[None, :]
    cos, sin = jnp.cos(ang), jnp.sin(ang)
    return jnp.concatenate([cos, cos, -sin, sin], axis=-1)


def _rope_cols(w, half):
    return jnp.concatenate([w, w[..., half:], w[..., :half]], axis=-1)


def _kvprep(a, kr_raw, tab, q_norm_g, kv_norm_g, ql, kvl, rd, tm, name):
    m = a.shape[0]

    def kern(a_ref, kr_ref, tab_ref, qg_ref, kg_ref, qn_ref, ckv_ref, kro_ref, kcat_ref):
        av = a_ref[...]
        cq, ck = av[:, :ql], av[:, ql:]
        qn = cq * lax.rsqrt(jnp.mean(cq * cq, axis=-1, keepdims=True) + RMS_EPS) * qg_ref[...]
        qn_ref[...] = qn.astype(BF16)
        ckv = ck * lax.rsqrt(jnp.mean(ck * ck, axis=-1, keepdims=True) + RMS_EPS) * kg_ref[...]
        ckv_ref[...] = ckv
        r = kr_ref[...] * tab_ref[...]
        rope = r + pltpu.roll(r, rd, 1)
        kro_ref[...] = rope[:, :rd]
        lane = lax.broadcasted_iota(jnp.int32, rope.shape, 1)
        kcat_ref[...] = jnp.concatenate([ckv, jnp.where(lane < rd, rope, 0.0)], axis=1).astype(BF16)

    return pl.pallas_call(
        kern,
        out_shape=[jax.ShapeDtypeStruct((m, ql), BF16), jax.ShapeDtypeStruct((m, kvl), F32),
                   jax.ShapeDtypeStruct((m, rd), F32), jax.ShapeDtypeStruct((m, kvl + 2 * rd), BF16)],
        grid=(m // tm,),
        in_specs=[pl.BlockSpec((tm, ql + kvl), lambda i: (i, 0)),
                  pl.BlockSpec((tm, 2 * rd), lambda i: (i, 0)),
                  pl.BlockSpec((tm, 2 * rd), lambda i: (i, 0)),
                  pl.BlockSpec((1, ql), lambda i: (0, 0)),
                  pl.BlockSpec((1, kvl), lambda i: (0, 0))],
        out_specs=[pl.BlockSpec((tm, ql), lambda i: (i, 0)), pl.BlockSpec((tm, kvl), lambda i: (i, 0)),
                   pl.BlockSpec((tm, rd), lambda i: (i, 0)), pl.BlockSpec((tm, kvl + 2 * rd), lambda i: (i, 0))],
        name=name, compiler_params=_params(("parallel",)),
    )(a, kr_raw, tab, q_norm_g.reshape(1, ql), kv_norm_g.reshape(1, kvl))


def _qproj(qn, w_q, w_uk_t, tab, tm, name):
    m, ql = qn.shape
    nh, nd, kvl = w_uk_t.shape
    rd2 = w_q.shape[2] - nd

    def kern(qn_ref, wq_ref, wuk_ref, tab_ref, o_ref):
        qq = _dot(qn_ref[...], wq_ref[...].astype(BF16))
        q_lat = _dot(qq[:, :nd].astype(BF16), wuk_ref[...].astype(BF16))
        r = qq[:, nd:] * tab_ref[...]
        q_rope = r + pltpu.roll(r, rd2 // 2, 1)
        o_ref[...] = jnp.concatenate([q_lat, q_rope], axis=1).astype(BF16)

    return pl.pallas_call(
        kern, out_shape=jax.ShapeDtypeStruct((nh, m, kvl + rd2), BF16), grid=(nh, m // tm),
        in_specs=[pl.BlockSpec((tm, ql), lambda h, i: (i, 0)),
                  pl.BlockSpec((None, ql, nd + rd2), lambda h, i: (h, 0, 0)),
                  pl.BlockSpec((None, nd, kvl), lambda h, i: (h, 0, 0)),
                  pl.BlockSpec((tm, rd2), lambda h, i: (i, 0))],
        out_specs=pl.BlockSpec((None, tm, kvl + rd2), lambda h, i: (h, i, 0)),
        name=name, compiler_params=_params(("parallel", "parallel")),
    )(qn, w_q, w_uk_t, tab)


def _prompt_attention(qcat, kcat, w_uv, n_seq, t_len, scale, tq, hb, name):
    nh, m, e = qcat.shape
    _, kvl, vd = w_uv.shape
    nq = t_len // tq
    heads = range(hb)

    def kern(q_ref, k_ref, w_ref, o_ref, m_ref, l_ref, acc_ref):
        qi = pl.program_id(2)
        m_ref[...] = jnp.full(m_ref.shape, -jnp.inf, F32)
        l_ref[...] = jnp.zeros(l_ref.shape, F32)
        acc_ref[...] = jnp.zeros(acc_ref.shape, F32)

        def update(j, masked):
            kb = k_ref[pl.ds(pl.multiple_of(j * tq, tq), tq), :]
            ss = [_dot_nt(q_ref[h], kb) * scale for h in heads]
            if masked:
                row = lax.broadcasted_iota(jnp.int32, (tq, tq), 0)
                col = lax.broadcasted_iota(jnp.int32, (tq, tq), 1)
                ss = [jnp.where(col <= row, s, -jnp.inf) for s in ss]
            ps, alphas = [], []
            for h in heads:
                m_prev = m_ref[h]
                m_new = jnp.maximum(m_prev, jnp.max(ss[h], axis=-1, keepdims=True))
                alpha = jnp.exp(m_prev - m_new)
                p = jnp.exp(ss[h] - m_new)
                l_ref[h] = l_ref[h] * alpha + jnp.sum(p, axis=-1, keepdims=True)
                m_ref[h] = m_new
                ps.append(p.astype(BF16))
                alphas.append(alpha)
            pvs = [_dot(ps[h], kb[:, :kvl]) for h in heads]
            for h in heads:
                acc_ref[h] = acc_ref[h] * alphas[h] + pvs[h]

        def body(j, carry):
            update(j, False)
            return carry

        lax.fori_loop(0, qi, body, 0)
        update(qi, True)
        for h in heads:
            o_lat = (acc_ref[h] / l_ref[h]).astype(BF16)
            o_ref[:, h * vd:(h + 1) * vd] = _dot(o_lat, w_ref[h].astype(BF16)).astype(BF16)

    return pl.pallas_call(
        kern, out_shape=jax.ShapeDtypeStruct((m, nh * vd), BF16), grid=(n_seq, nh // hb, nq),
        in_specs=[pl.BlockSpec((hb, tq, e), lambda b, hg, qi: (hg, b * nq + qi, 0)),
                  pl.BlockSpec((t_len, e), lambda b, hg, qi: (b, 0)),
                  pl.BlockSpec((hb, kvl, vd), lambda b, hg, qi: (hg, 0, 0))],
        out_specs=pl.BlockSpec((tq, hb * vd), lambda b, hg, qi: (b * nq + qi, hg)),
        scratch_shapes=[pltpu.VMEM((hb, tq, 1), F32), pltpu.VMEM((hb, tq, 1), F32),
                        pltpu.VMEM((hb, tq, kvl), F32)],
        name=name, compiler_params=_params(("parallel", "parallel", "arbitrary")),
    )(qcat, kcat, w_uv)


def _decode_attention(q, cache_kv, cache_kr_t, layer, page_table, ckv_new, kr_new, scale, pp, name):
    bsz, nh, e = q.shape
    page, kvl = cache_kv.shape[2], cache_kv.shape[3]
    rd = cache_kr_t.shape[2]
    n_pages = page_table.shape[1]
    assert n_pages % pp == 0
    ns = n_pages // pp

    def kern(pt_ref, q_ref, *refs):
        kv_refs = refs[:pp]
        kr_refs = refs[pp:2 * pp]
        cn_ref, rn_ref, o_ref, m_ref, l_ref, acc_ref, kvb_ref, krb_ref = refs[2 * pp:]
        s_idx = pl.program_id(1)

        @pl.when(s_idx == 0)
        def _():
            m_ref[...] = jnp.full(m_ref.shape, -jnp.inf, F32)
            l_ref[...] = jnp.zeros(l_ref.shape, F32)
            acc_ref[...] = jnp.zeros(acc_ref.shape, F32)

        qv = q_ref[0]
        ql, qr = qv[:, :kvl], qv[:, kvl:kvl + rd]

        def online(s, pv_fn):
            m_prev = m_ref[...]
            m_new = jnp.maximum(m_prev, jnp.max(s, axis=-1, keepdims=True))
            alpha = jnp.exp(m_prev - m_new)
            p = jnp.exp(s - m_new)
            l_ref[...] = l_ref[...] * alpha + jnp.sum(p, axis=-1, keepdims=True)
            acc_ref[...] = acc_ref[...] * alpha + pv_fn(p)
            m_ref[...] = m_new

        for j in range(pp):
            kvb_ref[j * page:(j + 1) * page, :] = kv_refs[j][...].astype(BF16)
            krb_ref[:, j * page:(j + 1) * page] = kr_refs[j][...].astype(BF16)
        kv_all = kvb_ref[...]
        s = (_dot_nt(ql, kv_all) + _dot(qr, krb_ref[...])) * scale
        online(s, lambda p: _dot(p.astype(BF16), kv_all))

        @pl.when(s_idx == ns - 1)
        def _():
            cn = cn_ref[0].astype(BF16).astype(F32)
            rn = rn_ref[0].astype(BF16).astype(F32)
            s_new = (jnp.sum(ql.astype(F32) * cn, axis=-1, keepdims=True)
                     + jnp.sum(qr.astype(F32) * rn, axis=-1, keepdims=True)) * scale
            online(s_new, lambda p: p.astype(BF16).astype(F32) * cn)
            o_ref[0] = acc_ref[...] / l_ref[...]

    in_specs = [pl.BlockSpec((1, nh, e), lambda b, s, pt: (b, 0, 0))]
    for j in range(pp):
        in_specs.append(pl.BlockSpec((None, None, page, kvl),
                                     lambda b, s, pt, j=j: (layer, pt[b, s * pp + j], 0, 0)))
    for j in range(pp):
        in_specs.append(pl.BlockSpec((None, None, rd, page),
                                     lambda b, s, pt, j=j: (layer, pt[b, s * pp + j], 0, 0)))
    in_specs.append(pl.BlockSpec((1, 1, kvl), lambda b, s, pt: (b, 0, 0)))
    in_specs.append(pl.BlockSpec((1, 1, rd), lambda b, s, pt: (b, 0, 0)))
    grid_spec = pltpu.PrefetchScalarGridSpec(
        num_scalar_prefetch=1, grid=(bsz, ns), in_specs=in_specs,
        out_specs=pl.BlockSpec((1, nh, kvl), lambda b, s, pt: (b, 0, 0)),
        scratch_shapes=[pltpu.VMEM((nh, 1), F32), pltpu.VMEM((nh, 1), F32), pltpu.VMEM((nh, kvl), F32),
                        pltpu.VMEM((pp * page, kvl), BF16), pltpu.VMEM((rd, pp * page), BF16)])
    return pl.pallas_call(
        kern, out_shape=jax.ShapeDtypeStruct((bsz, nh, kvl), F32), grid_spec=grid_spec, name=name,
        compiler_params=_params(("parallel", "arbitrary")),
    )(page_table, q, *([cache_kv] * pp), *([cache_kr_t] * pp), ckv_new, kr_new)


def _oproj(o_lat, w_uv, tm, name):
    nh, m, kvl = o_lat.shape
    vd = w_uv.shape[2]

    def kern(o_ref, w_ref, y_ref):
        y_ref[...] = _dot(o_ref[...].astype(BF16), w_ref[...].astype(BF16)).astype(BF16)

    return pl.pallas_call(
        kern, out_shape=jax.ShapeDtypeStruct((m, nh * vd), BF16), grid=(nh, m // tm),
        in_specs=[pl.BlockSpec((None, tm, kvl), lambda h, i: (h, i, 0)),
                  pl.BlockSpec((None, kvl, vd), lambda h, i: (h, 0, 0))],
        out_specs=pl.BlockSpec((tm, vd), lambda h, i: (i, h)),
        name=name, compiler_params=_params(("parallel", "parallel")),
    )(o_lat, w_uv)


def _conv_prompt(bcu, conv_w, n_seq, t_len, cd, tc, name):
    m = bcu.shape[0]
    ncb = cd // tc
    cw = conv_w.shape[0]

    def kern(b_ref, c_ref, u_ref, w_ref, y_ref, st_ref):
        xg = c_ref[...] * u_ref[...]
        w = w_ref[...]
        row = lax.broadcasted_iota(jnp.int32, xg.shape, 0)
        y = xg * w[cw - 1:cw]
        for d in range(1, cw):
            sh = jnp.where(row >= d, pltpu.roll(xg, d, 0), 0.0)
            y = y + sh * w[cw - 1 - d:cw - d]
        y_ref[...] = (b_ref[...] * y).astype(BF16)
        st_ref[0] = xg[t_len - (cw - 1):, :]

    return pl.pallas_call(
        kern, out_shape=[jax.ShapeDtypeStruct((m, cd), BF16), jax.ShapeDtypeStruct((n_seq, cw - 1, cd), F32)],
        grid=(n_seq, ncb),
        in_specs=[pl.BlockSpec((t_len, tc), lambda b, j: (b, j)),
                  pl.BlockSpec((t_len, tc), lambda b, j: (b, ncb + j)),
                  pl.BlockSpec((t_len, tc), lambda b, j: (b, 2 * ncb + j)),
                  pl.BlockSpec((cw, tc), lambda b, j: (0, j))],
        out_specs=[pl.BlockSpec((t_len, tc), lambda b, j: (b, j)),
                   pl.BlockSpec((1, cw - 1, tc), lambda b, j: (b, 0, j))],
        name=name, compiler_params=_params(("parallel", "parallel")),
    )(bcu, bcu, bcu, conv_w)


def _conv_step(bcu, prev, conv_w, cd, tc, name):
    bsz = bcu.shape[0]
    ncb = cd // tc
    cw = conv_w.shape[0]

    def kern(*refs):
        b_ref, c_ref, u_ref, w_ref = refs[:4]
        p_refs = refs[4:4 + cw - 1]
        y_ref = refs[4 + cw - 1]
        st_refs = refs[5 + cw - 1:]
        xg = c_ref[...] * u_ref[...]
        w = w_ref[...]
        y = xg * w[cw - 1:cw]
        for d in range(cw - 1):
            y = y + p_refs[d][...] * w[d:d + 1]
        y_ref[...] = (b_ref[...] * y).astype(BF16)
        for d in range(cw - 2):
            st_refs[d][...] = p_refs[d + 1][...]
        st_refs[cw - 2][...] = xg

    in_specs = [pl.BlockSpec((bsz, tc), lambda j: (0, j)),
                pl.BlockSpec((bsz, tc), lambda j: (0, ncb + j)),
                pl.BlockSpec((bsz, tc), lambda j: (0, 2 * ncb + j)),
                pl.BlockSpec((cw, tc), lambda j: (0, j))]
    for d in range(cw - 1):
        in_specs.append(pl.BlockSpec((bsz, tc), lambda j, d=d: (0, d * ncb + j)))
    outs = pl.pallas_call(
        kern,
        out_shape=[jax.ShapeDtypeStruct((bsz, cd), BF16)] + [jax.ShapeDtypeStruct((bsz, cd), F32)] * (cw - 1),
        grid=(ncb,), in_specs=in_specs,
        out_specs=[pl.BlockSpec((bsz, tc), lambda j: (0, j))] * cw,
        name=name, compiler_params=_params(("parallel",)),
    )(bcu, bcu, bcu, conv_w, *([prev] * (cw - 1)))
    return outs[0], jnp.stack(outs[1:], axis=1)


def _token_mix(h3, prev3, mu, tc, name):
    b, t, d = h3.shape
    nmix = mu.shape[0]
    shift_here = prev3 is None

    def kern(*refs):
        if shift_here:
            h_ref, mu_ref = refs[:2]
            o_refs = refs[2:]
        else:
            h_ref, p_ref, mu_ref = refs[:3]
            o_refs = refs[3:]
        hh = h_ref[0]
        if shift_here:
            row = lax.broadcasted_iota(jnp.int32, hh.shape, 0)
            prev = jnp.where(row >= 1, pltpu.roll(hh, 1, 0), 0.0)
        else:
            prev = p_ref[0]
        xx = prev - hh
        mv = mu_ref[...]
        for i in range(nmix):
            o_refs[i][0] = (hh + xx * mv[i:i + 1]).astype(BF16)

    blk = pl.BlockSpec((1, t, tc), lambda bi, j: (bi, 0, j))
    in_specs = [blk] + ([] if shift_here else [blk]) + [pl.BlockSpec((nmix, tc), lambda bi, j: (0, j))]
    args = [h3] + ([] if shift_here else [prev3]) + [mu]
    return pl.pallas_call(
        kern, out_shape=[jax.ShapeDtypeStruct((b, t, d), BF16)] * nmix, grid=(b, d // tc),
        in_specs=in_specs, out_specs=[blk] * nmix,
        name=name, compiler_params=_params(("parallel", "parallel")),
    )(*args)


def _wkv_head_prep(r, k, a, kkw, kaw):
    kk = k * kkw
    kk = kk * lax.rsqrt(jnp.maximum(jnp.sum(kk * kk, axis=-1, keepdims=True), 1e-24))
    kmod = k * (1.0 + (a - 1.0) * kaw)
    return kk, kmod


def _wkv_head_post(y, r, kmod, v, g, rkw, lnw, lnb):
    mean = jnp.mean(y, axis=-1, keepdims=True)
    var = jnp.mean(jnp.square(y - mean), axis=-1, keepdims=True)
    yn = (y - mean) * lax.rsqrt(var + GN_EPS) * lnw + lnb
    bonus = jnp.sum(r * kmod * rkw, axis=-1, keepdims=True) * v
    return (yn + bonus) * g


def _wkv_prompt(r, k, v, a, lw, g, k_k, k_a, r_k, ln_w, ln_b, n_seq, t_len, hs, name):
    m, d = r.shape
    c = WKV_CHUNK
    hl = min(WKV_LANES, d)
    hpb = hl // hs
    nh = d // hs
    nc = t_len // c

    def kern(r_ref, k_ref, v_ref, a_ref, lw_ref, g_ref, kk_ref, ka_ref, rk_ref, lnw_ref, lnb_ref,
             o_ref, s_ref):
        ci = pl.program_id(2)

        @pl.when(ci == 0)
        def _():
            s_ref[...] = jnp.zeros(s_ref.shape, F32)

        ri = lax.broadcasted_iota(jnp.int32, (c, c), 0)
        cj = lax.broadcasted_iota(jnp.int32, (c, c), 1)
        strict = cj < ri
        ri2 = lax.broadcasted_iota(jnp.int32, (c, 2 * c), 0)
        cj2 = lax.broadcasted_iota(jnp.int32, (c, 2 * c), 1)
        incl2 = jnp.where(cj2 >= c, cj2 - c, cj2) <= ri2
        tri = jnp.where(cj <= ri, 1.0, 0.0).astype(BF16)
        eye = jnp.where(cj == ri, 1.0, 0.0).astype(F32)
        heads = range(hpb)
        sls = [slice(hh * hs, (hh + 1) * hs) for hh in heads]

        lw_all = lw_ref[...]
        lw_hi = lw_all.astype(BF16)
        lw_lo = (lw_all - lw_hi.astype(F32)).astype(BF16)
        cum_all = _dot(tri, lw_hi) + _dot(tri, lw_lo)
        at, rt, bk_t, bk_h, vb, kmods, gtot = [], [], [], [], [], [], []
        for hh in heads:
            sl = sls[hh]
            rr, k0, aa, lwh = r_ref[:, sl], k_ref[:, sl], a_ref[:, sl], lw_all[:, sl]
            kk, kmod = _wkv_head_prep(rr, k0, aa, kk_ref[:, sl], ka_ref[:, sl])
            avec, bvec = -kk, kk * aa
            cum = cum_all[:, sl]
            tot = cum[c - 1:c, :]
            at.append((avec * jnp.exp(cum - lwh)).astype(BF16))
            rt.append((rr * jnp.exp(cum)).astype(BF16))
            e_neg = jnp.exp(-cum)
            e_tot = jnp.exp(tot - cum)
            bk_t.append(jnp.concatenate([bvec * e_neg, kmod * e_neg], axis=0).astype(BF16))
            bk_h.append(jnp.concatenate([bvec * e_tot, kmod * e_tot], axis=0).astype(BF16))
            vb.append(v_ref[:, sl].astype(BF16))
            kmods.append(kmod)
            gtot.append(jnp.exp(tot))
        a_ab = [jnp.where(strict, _dot_nt(at[hh], bk_t[hh][:c]), 0.0) for hh in heads]
        a_ak = [jnp.where(strict, _dot_nt(at[hh], bk_t[hh][c:]), 0.0).astype(BF16) for hh in heads]
        a_r = [jnp.where(incl2, _dot_nt(rt[hh], bk_t[hh]), 0.0).astype(BF16) for hh in heads]
        s0 = [s_ref[0, hh] for hh in heads]
        ws = [_dot_nt(jnp.concatenate([at[hh], rt[hh]], axis=0), s0[hh].astype(BF16)) for hh in heads]
        akv = [_dot(a_ak[hh], vb[hh]) for hh in heads]
        tinv = [eye + a_ab[hh] for hh in heads]
        lp = [a_ab[hh].astype(BF16) for hh in heads]
        n_done = 2
        while n_done < c:
            lp = [_dot(lp[hh], lp[hh]).astype(BF16) for hh in heads]
            tinv = [tinv[hh] + _dot(tinv[hh].astype(BF16), lp[hh]) for hh in heads]
            n_done *= 2
        p = [_dot(tinv[hh].astype(BF16), (ws[hh][:c] + akv[hh]).astype(BF16)) for hh in heads]
        z = [jnp.concatenate([p[hh].astype(BF16), vb[hh]], axis=0) for hh in heads]
        y = [ws[hh][c:] + _dot(a_r[hh], z[hh]) for hh in heads]
        s1 = [s0[hh] * gtot[hh] + _dot_tn(z[hh], bk_h[hh]) for hh in heads]
        for hh in heads:
            sl = sls[hh]
            s_ref[0, hh] = s1[hh]
            o_ref[:, sl] = _wkv_head_post(y[hh], r_ref[:, sl], kmods[hh], v_ref[:, sl], g_ref[:, sl],
                                          rk_ref[:, sl], lnw_ref[:, sl], lnb_ref[:, sl]).astype(BF16)

    blk = pl.BlockSpec((c, hl), lambda b, hg, ci: (b * nc + ci, hg))
    wblk = pl.BlockSpec((1, hl), lambda b, hg, ci: (0, hg))
    return pl.pallas_call(
        kern, out_shape=[jax.ShapeDtypeStruct((m, d), BF16), jax.ShapeDtypeStruct((n_seq, nh, hs, hs), F32)],
        grid=(n_seq, d // hl, nc),
        in_specs=[blk] * 6 + [wblk] * 5,
        out_specs=[blk, pl.BlockSpec((1, hpb, hs, hs), lambda b, hg, ci: (b, hg, 0, 0))],
        name=name, compiler_params=_params(("parallel", "parallel", "arbitrary")),
    )(r, k, v, a, lw, g, k_k, k_a, r_k, ln_w, ln_b)


def _wkv_step(r, k, a, lw, v, g, s0, k_k, k_a, r_k, ln_w, ln_b, hb, name):
    nh, hs, bsz = r.shape

    def kern(r_ref, k_ref, a_ref, lw_ref, v_ref, g_ref, s_ref, kk_ref, ka_ref, rk_ref, lnw_ref, lnb_ref,
             o_ref, so_ref):
        rr, k0, aa = r_ref[...], k_ref[...], a_ref[...]
        vv = v_ref[...]
        dec = jnp.exp(lw_ref[...])
        kk = k0 * kk_ref[...]
        kk = kk * lax.rsqrt(jnp.maximum(jnp.sum(kk * kk, axis=1, keepdims=True), 1e-24))
        kmod = k0 * (1.0 + (aa - 1.0) * ka_ref[...])
        avec, bvec = -kk, kk * aa
        s = s_ref[...]
        sa = jnp.sum(s * avec[:, None], axis=2, keepdims=True)
        s1 = s * dec[:, None] + sa * bvec[:, None] + vv * kmod[:, None]
        so_ref[...] = s1
        y = jnp.sum(s1 * rr[:, None], axis=2, keepdims=True)
        mean = jnp.mean(y, axis=1, keepdims=True)
        var = jnp.mean(jnp.square(y - mean), axis=1, keepdims=True)
        yn = (y - mean) * lax.rsqrt(var + GN_EPS) * lnw_ref[...] + lnb_ref[...]
        bonus = jnp.sum(rr * kmod * rk_ref[...], axis=1, keepdims=True)[:, None] * vv
        o_ref[...] = (yn + bonus) * g_ref[...]

    kblk = pl.BlockSpec((hb, hs, bsz), lambda i: (i, 0, 0))
    vblk = pl.BlockSpec((hb, hs, 1, bsz), lambda i: (i, 0, 0, 0))
    sblk = pl.BlockSpec((hb, hs, hs, bsz), lambda i: (i, 0, 0, 0))
    wkblk = pl.BlockSpec((hb, hs, 1), lambda i: (i, 0, 0))
    wvblk = pl.BlockSpec((hb, hs, 1, 1), lambda i: (i, 0, 0, 0))
    return pl.pallas_call(
        kern, out_shape=[jax.ShapeDtypeStruct((nh, hs, 1, bsz), F32), jax.ShapeDtypeStruct((nh, hs, hs, bsz), F32)],
        grid=(nh // hb,),
        in_specs=[kblk] * 4 + [vblk] * 2 + [sblk] + [wkblk] * 3 + [wvblk] * 2,
        out_specs=[vblk, sblk],
        name=name, compiler_params=_params(("parallel",)),
    )(r, k, a, lw, v, g, s0, k_k, k_a, r_k, ln_w, ln_b)


def _pick(n, candidates):
    for c in candidates:
        if n % c == 0:
            return c
    return n


def kernel(x_prompt, x_sample, cache_kv_latent, cache_k_rope, state_conv, state_rwkv_shift, state_rwkv_wkv, page_table, c_prompt, c_sample, w_ada, b_ada, norm_g, ffn_w1, ffn_w3, ffn_w2, mla_w_in, mla_q_norm, mla_kv_norm, mla_w_uq, mla_w_ukv, conv_w, mix_w_out, rwkv_mu, rwkv_w0, rwkv_w1, rwkv_w2, rwkv_a0, rwkv_a1, rwkv_a2, rwkv_g1, rwkv_g2, rwkv_k_k, rwkv_k_a, rwkv_r_k, rwkv_w_r, rwkv_w_k, rwkv_w_v, rwkv_w_o, rwkv_ln_w, rwkv_ln_b, final_norm_g):
    nb, seq, d = x_prompt.shape
    nsb, dec_seq, _ = x_sample.shape
    assert dec_seq == 1
    depth = w_ada.shape[0]
    dff = ffn_w1.shape[-1]
    ql = mla_q_norm.shape[1]
    kvl = mla_kv_norm.shape[1]
    rd = cache_k_rope.shape[-1]
    nh_mla = mla_w_ukv.shape[2]
    nd = mla_w_uq.shape[2] // nh_mla - rd
    vd = mla_w_ukv.shape[3] - nd
    cd = conv_w.shape[-1]
    cw = conv_w.shape[1]
    hs = state_rwkv_wkv.shape[-1]
    nh_rwkv = d // hs
    past_len = page_table.shape[1] * cache_kv_latent.shape[2]
    mla_scale = float((nd + rd) ** -0.5)
    mp = nb * seq

    n_c = nsb + nb
    n_c_pad = -(-n_c // 16) * 16
    c_all = jnp.concatenate([c_sample, c_prompt, jnp.zeros((n_c_pad - n_c, d), F32)], axis=0)
    t_ada = _pick(9 * d, (512, 256, 128))
    adas = []
    for layer in range(depth):
        (ada,) = _mm([c_all], [_w(w_ada, (layer,))], [[(0, 0)]],
                     lambda acc, ev: (acc[0] + ev[0],), [F32],
                     n=9 * d, tm=n_c_pad, tn=t_ada, x_pro=_silu,
                     extras=[_ex_col(b_ada[layer].reshape(1, 9 * d), t_ada)], name=f"ada_l{layer}")
        adas.append(ada)

    def run_group(x2, is_sample):
        m = x2.shape[0]
        if is_sample:
            n_seq, t_len = m, 1
            xshape = (1, m, d)
            tm = m
            tt = m
            pos = jnp.full((m,), past_len, jnp.int32)
        else:
            n_seq, t_len = nb, seq
            xshape = (nb, seq, d)
            tm = _pick(seq, (1024, 512, 256, 128))
            tt = _pick(seq, (256, 128))
            pos = jnp.tile(jnp.arange(seq, dtype=jnp.int32), nb)
        tn = _pick(d, (512, 256, 128))
        tab = _rope_tables(pos, rd // 2)
        kv_rows, kr_rows, conv_states, shift_states, wkv_states = [], [], [], [], []

        def ffn(x2, mod, layer, idx, sub, tag):
            h = _adanorm(x2.reshape(xshape), norm_g[layer, sub], mod, sub, BF16, tt, f"norm_{tag}").reshape(m, d)
            tnf = _pick(dff, (256, 128))
            (gact,) = _mm([h], [_w(ffn_w1, (layer, idx)), _w(ffn_w3, (layer, idx))], [[(0, 0)], [(0, 1)]],
                          lambda acc, ev: (_silu(acc[0]) * acc[1],), [BF16],
                          n=dff, tm=tm, tn=tnf, name=f"ffn_up_{tag}")
            tkd = _pick(dff, (256, 128)) if not is_sample else dff
            tnd = _pick(d, (1024, 512, 256, 128)) if not is_sample else _pick(d, (256, 128))
            (xo,) = _mm([gact], [_w(ffn_w2, (layer, idx))], [[(0, 0)]],
                        lambda acc, ev: (ev[0] + FFN_RES * ev[1] * acc[0],), [F32],
                        n=d, tm=tm, tn=tnd, tk=tkd, kq=1 if is_sample else 4,
                        extras=[_ex_row(x2, tm, tnd), mod.mm_extra(sub, 2, tm, tnd)], name=f"ffn_down_{tag}")
            return xo

        for layer in range(depth):
            tag = f"{'s' if is_sample else 'p'}{layer}"
            mod = _Mod(adas[layer], d, is_sample, nsb, t_len)
            x2 = ffn(x2, mod, layer, 0, 0, tag + "a")
            if layer % 2 == 0:
                i = layer // 2
                h = _adanorm(x2.reshape(xshape), norm_g[layer, 1], mod, 1, BF16, tt, f"norm_mix_{tag}").reshape(m, d)
                w_in = mla_w_in[i]
                n_a = ql + kvl
                (a_proj,) = _mm([h], [_w(w_in)], [[(0, 0)]], lambda acc, ev: (acc[0],), [F32],
                                n=n_a, tm=tm, tn=_pick(n_a, (512, 256, 128)), name=f"mla_in_a_{tag}")
                w_kr = _rope_cols(w_in[:, n_a:n_a + rd], rd // 2)
                (kr_raw,) = _mm([h], [_w(w_kr)], [[(0, 0)]], lambda acc, ev: (acc[0],), [F32],
                                n=2 * rd, tm=tm, tn=2 * rd, name=f"mla_in_kr_{tag}")
                w_bcu = w_in[:, n_a + rd:]
                (bcu,) = _mm([h], [_w(w_bcu)], [[(0, 0)]], lambda acc, ev: (acc[0],), [F32],
                             n=3 * cd, tm=tm, tn=_pick(cd, (512, 256, 128)), name=f"mla_in_bcu_{tag}")
                qn, ckv, kr, kcat = _kvprep(a_proj, kr_raw, tab, mla_q_norm[i], mla_kv_norm[i], ql, kvl, rd,
                                            _pick(m, (256, 128)), f"kvprep_{tag}")
                w_uq = mla_w_uq[i].reshape(ql, nh_mla, nd + rd)
                w_uq_h = jnp.transpose(w_uq, (1, 0, 2))
                w_q = jnp.concatenate([w_uq_h[:, :, :nd], _rope_cols(w_uq_h[:, :, nd:], rd // 2)], axis=-1)
                w_uk_t = jnp.transpose(mla_w_ukv[i][:, :, :nd], (1, 2, 0))
                w_uv = jnp.transpose(mla_w_ukv[i][:, :, nd:], (1, 0, 2))
                qcat = _qproj(qn, w_q, w_uk_t, tab, _pick(m, (1024, 512, 256, 128)), f"qproj_{tag}")
                if is_sample:
                    o_lat = _decode_attention(jnp.transpose(qcat, (1, 0, 2)), cache_kv_latent,
                                              jnp.swapaxes(cache_k_rope, 2, 3), i,
                                              page_table, ckv.reshape(m, 1, kvl), kr.reshape(m, 1, rd),
                                              mla_scale, _pick(page_table.shape[1], (16, 8, 4, 2, 1)), f"attn_{tag}")
                    o_att = _oproj(jnp.transpose(o_lat, (1, 0, 2)), w_uv, _pick(m, (512, 256, 128)), f"oproj_{tag}")
                    prev = state_conv[i].reshape(m, (cw - 1) * cd)
                    conv_y, conv_new = _conv_step(bcu, prev, conv_w[i], cd, _pick(cd, (512, 256, 128)), f"conv_{tag}")
                else:
                    o_att = _prompt_attention(qcat, kcat, w_uv, n_seq, t_len, mla_scale,
                                              _pick(t_len, (ATTN_TQ, 128)), _pick(nh_mla, (ATTN_HEADS, 2, 1)),
                                              f"attn_{tag}")
                    conv_y, conv_new = _conv_prompt(bcu, conv_w[i], n_seq, t_len, cd, _pick(cd, (256, 128)),
                                                    f"conv_{tag}")
                assert nh_mla * vd == cd
                (x2,) = _mm([o_att, conv_y], [_w(mix_w_out, (i,)), _w(mix_w_out, (i,), roff=1)], [[(0, 0), (1, 1)]],
                            lambda acc, ev: (ev[0] + ev[1] * acc[0],), [F32],
                            n=d, tm=tm, tn=tn, extras=[_ex_row(x2, tm, tn), mod.mm_extra(1, 2, tm, tn)],
                            name=f"mix_out_{tag}")
                kv_rows.append(ckv.reshape(n_seq, t_len, kvl))
                kr_rows.append(kr.reshape(n_seq, t_len, rd))
                conv_states.append(conv_new)
            else:
                j = layer // 2
                h3 = _adanorm(x2.reshape(xshape), norm_g[layer, 1], mod, 1, F32, tt, f"norm_mix_{tag}")
                tcm = _pick(d, (256, 128))
                if is_sample:
                    mixes = _token_mix(h3, state_rwkv_shift[j].reshape(xshape), rwkv_mu[j], tcm, f"tmix_{tag}")
                    shift_states.append(h3.reshape(m, d))
                else:
                    mixes = _token_mix(h3, None, rwkv_mu[j], tcm, f"tmix_{tag}")
                    shift_states.append(h3[:, -1])
                xr, xw, xk, xv, xa, xg = (t.reshape(m, d) for t in mixes)

                def proj(x, w, tag2):
                    (o,) = _mm([x], [_w(w, (j,))], [[(0, 0)]], lambda acc, ev: (acc[0],), [F32],
                               n=d, tm=tm, tn=tn, name=f"rwkv_{tag2}_{tag}")
                    return o

                r, k, v = proj(xr, rwkv_w_r, "r"), proj(xk, rwkv_w_k, "k"), proj(xv, rwkv_w_v, "v")
                dl = rwkv_w1.shape[-1]
                (t1,) = _mm([xw], [_w(rwkv_w1, (j,))], [[(0, 0)]], lambda acc, ev: (jnp.tanh(acc[0]),), [BF16],
                            n=dl, tm=tm, tn=dl, name=f"rwkv_w1_{tag}")

                def decay_epi(acc, ev):
                    z = -(ev[0] + acc[0])
                    softplus = jnp.maximum(z, 0.0) + jnp.log(1.0 + jnp.exp(-jnp.abs(z)))
                    return (-jnp.exp(-softplus - 0.5),)

                (lw,) = _mm([t1], [_w(rwkv_w2, (j,))], [[(0, 0)]], decay_epi, [F32],
                            n=d, tm=tm, tn=tn, extras=[_ex_col(rwkv_w0[j].reshape(1, d), tn)], name=f"rwkv_w2_{tag}")
                al = rwkv_a1.shape[-1]
                (t2,) = _mm([xa], [_w(rwkv_a1, (j,))], [[(0, 0)]], lambda acc, ev: (acc[0],), [BF16],
                            n=al, tm=tm, tn=al, name=f"rwkv_a1_{tag}")
                (a,) = _mm([t2], [_w(rwkv_a2, (j,))], [[(0, 0)]], lambda acc, ev: (_sigmoid(ev[0] + acc[0]),), [F32],
                           n=d, tm=tm, tn=tn, extras=[_ex_col(rwkv_a0[j].reshape(1, d), tn)], name=f"rwkv_a2_{tag}")
                gl = rwkv_g1.shape[-1]
                glp = -(-gl // 128) * 128
                g1p = jnp.pad(rwkv_g1[j], ((0, 0), (0, glp - gl)))
                g2p = jnp.pad(rwkv_g2[j], ((0, glp - gl), (0, 0)))
                (t3,) = _mm([xg], [_w(g1p)], [[(0, 0)]], lambda acc, ev: (_sigmoid(acc[0]),), [BF16],
                            n=glp, tm=tm, tn=glp, name=f"rwkv_g1_{tag}")
                (g,) = _mm([t3], [_w(g2p)], [[(0, 0)]], lambda acc, ev: (acc[0],), [F32],
                           n=d, tm=tm, tn=tn, name=f"rwkv_g2_{tag}")
                if is_sample:
                    tk3 = lambda t: t.T.reshape(nh_rwkv, hs, m)
                    tv4 = lambda t: t.T.reshape(nh_rwkv, hs, 1, m)
                    wk3 = lambda t: t.reshape(nh_rwkv, hs, 1)
                    wv4 = lambda t: t.reshape(nh_rwkv, hs, 1, 1)
                    o4, s_t = _wkv_step(tk3(r), tk3(k), tk3(a), tk3(lw), tv4(v), tv4(g),
                                        jnp.transpose(state_rwkv_wkv[j], (1, 2, 3, 0)),
                                        wk3(rwkv_k_k[j]), wk3(rwkv_k_a[j]), wk3(rwkv_r_k[j]),
                                        wv4(rwkv_ln_w[j]), wv4(rwkv_ln_b[j]),
                                        _pick(nh_rwkv, (2, 1)), f"wkv_{tag}")
                    o_pre = o4.reshape(d, m).T
                    s_new = jnp.transpose(s_t, (3, 0, 1, 2))
                else:
                    w2d = lambda t: t.reshape(1, d)
                    o_pre, s_new = _wkv_prompt(r, k, v, a, lw, g, w2d(rwkv_k_k[j]), w2d(rwkv_k_a[j]),
                                               w2d(rwkv_r_k[j]), w2d(rwkv_ln_w[j]), w2d(rwkv_ln_b[j]),
                                               n_seq, t_len, hs, f"wkv_{tag}")
                (x2,) = _mm([o_pre], [_w(rwkv_w_o, (j,))], [[(0, 0)]],
                            lambda acc, ev: (ev[0] + ev[1] * acc[0],), [F32],
                            n=d, tm=tm, tn=tn, extras=[_ex_row(x2, tm, tn), mod.mm_extra(1, 2, tm, tn)],
                            name=f"rwkv_o_{tag}")
                wkv_states.append(s_new)
            x2 = ffn(x2, mod, layer, 1, 2, tag + "b")
        y = _adanorm(x2.reshape(xshape), final_norm_g, None, 0, F32, tt, f"final_norm_{'s' if is_sample else 'p'}")
        return (y.reshape(n_seq, t_len, d), jnp.stack(kv_rows), jnp.stack(kr_rows), jnp.stack(conv_states),
                jnp.stack(shift_states), jnp.stack(wkv_states))

    y_p, p_kv, p_kr, p_conv, p_shift, p_wkv = run_group(x_prompt.reshape(mp, d), False)
    y_s, s_kv, s_kr, s_conv, s_shift, s_wkv = run_group(x_sample.reshape(nsb, d), True)
    return (y_p, y_s, p_kv, p_kr, p_conv, p_shift, p_wkv, s_kv, s_kr, s_conv, s_shift, s_wkv)
```

```python
import functools

import jax
import jax.numpy as jnp
from jax import lax
from jax.experimental import pallas as pl
from jax.experimental.pallas import tpu as pltpu

F32 = jnp.float32
BF16 = jnp.bfloat16

RMS_EPS = 1e-6
GN_EPS = 64e-5
ROPE_THETA = 10000.0
FFN_RES = 0.5
WKV_CHUNK = 64
WKV_LANES = 1024
ATTN_TQ = 512
ATTN_HEADS = 4
V7X_VMEM_LIMIT_BYTES = 56 * 1024 * 1024
HI = lax.Precision.HIGHEST


def _params(sem):
    return pltpu.CompilerParams(dimension_semantics=sem, vmem_limit_bytes=V7X_VMEM_LIMIT_BYTES)


def _dot(a, b, precision=None):
    return jnp.dot(a, b, preferred_element_type=F32, precision=precision)


def _dot_nt(a, b, precision=None):
    return lax.dot_general(a, b, (((1,), (1,)), ((), ())), preferred_element_type=F32, precision=precision)


def _dot_tn(a, b, precision=None):
    return lax.dot_general(a, b, (((0,), (0,)), ((), ())), preferred_element_type=F32, precision=precision)


def _sigmoid(x):
    return 1.0 / (1.0 + jnp.exp(-x))


def _silu(x):
    return x * _sigmoid(x)


def _w(arr, lead=(), roff=0, coff=0):
    return dict(arr=arr, lead=tuple(lead), roff=roff, coff=coff)


def _mm(xs, ws, accs, epi, out_dtypes, *, n, tm, tn, tk=None, kq=1, extras=(), x_pro=None, cast_ws=(), name):
    m, kdim = xs[0].shape
    tk = kdim if tk is None else tk
    assert m % tm == 0 and kdim % tk == 0, (name, m, tm, kdim, tk)
    assert not cast_ws or (m == tm and kq == 1)
    nchunk = kdim // tk
    nk = pl.cdiv(nchunk, kq)
    ragged = nk * kq != nchunk
    grid = (m // tm, pl.cdiv(n, tn), nk)
    nx, nw, ne, na, no = len(xs), len(ws), len(extras), len(accs), len(out_dtypes)

    def chunk(k, p):
        c = k * kq + p
        return jnp.minimum(c, nchunk - 1) if ragged else c

    in_specs = []
    for _ in xs:
        for p in range(kq):
            in_specs.append(pl.BlockSpec((tm, tk), lambda i, j, k, p=p: (i, chunk(k, p))))
    for w in ws:
        nlead = len(w["lead"])
        assert w["arr"].ndim == nlead + 2
        for p in range(kq):
            def w_map(i, j, k, w=w, p=p):
                return w["lead"] + (w["roff"] + chunk(k, p), w["coff"] + j)

            in_specs.append(pl.BlockSpec((None,) * nlead + (tk, tn), w_map))
    for (_, bshape, imap) in extras:
        in_specs.append(pl.BlockSpec(bshape, imap))
    out_shapes = [jax.ShapeDtypeStruct((m, n), dt) for dt in out_dtypes]
    out_specs = [pl.BlockSpec((tm, tn), lambda i, j, k: (i, j)) for _ in out_dtypes]
    for wi in cast_ws:
        out_shapes.append(jax.ShapeDtypeStruct(ws[wi]["arr"].shape[-2:], BF16))
        out_specs.append(pl.BlockSpec((tk, tn), lambda i, j, k: (k, j)))
    nc = len(cast_ws)

    def kern(*refs):
        x_refs = refs[:nx * kq]
        w_refs = refs[nx * kq:(nx + nw) * kq]
        e_refs = refs[(nx + nw) * kq:(nx + nw) * kq + ne]
        o_refs = refs[(nx + nw) * kq + ne:(nx + nw) * kq + ne + no]
        c_refs = refs[(nx + nw) * kq + ne + no:(nx + nw) * kq + ne + no + nc]
        acc_refs = refs[(nx + nw) * kq + ne + no + nc:]
        k = pl.program_id(2)
        xv = []
        for xi in range(nx):
            pieces = []
            for p in range(kq):
                v = x_refs[xi * kq + p][...]
                if x_pro is not None:
                    v = x_pro(v.astype(F32))
                v = v.astype(BF16)
                if ragged and (nk - 1) * kq + p >= nchunk:
                    v = jnp.where(k * kq + p < nchunk, v, jnp.zeros_like(v))
                pieces.append(v)
            xv.append(pieces)
        wv = [[w_refs[wi * kq + p][...].astype(BF16) for p in range(kq)] for wi in range(nw)]
        for c_ref, wi in zip(c_refs, cast_ws):
            c_ref[...] = wv[wi][0]

        def accumulate(init):
            outs = []
            for ai, acc in enumerate(accs):
                tot = None if init is None else init[ai]
                for (xi, wi) in acc:
                    for p in range(kq):
                        d = _dot(xv[xi][p], wv[wi][p])
                        tot = d if tot is None else tot + d
                outs.append(tot)
            return outs

        def finalize(vals):
            ev = []
            for r in e_refs:
                v = r[...]
                ev.append(v.reshape(v.shape[-2:]))
            outs = epi(vals, ev)
            for o_ref, o in zip(o_refs, outs):
                o_ref[...] = o.astype(o_ref.dtype).reshape(o_ref.shape)

        if nk == 1:
            finalize(accumulate(None))
        else:
            @pl.when(k == 0)
            def _():
                for a_ref, p in zip(acc_refs, accumulate(None)):
                    a_ref[...] = p

            if nk > 2:
                @pl.when(jnp.logical_and(k > 0, k < nk - 1))
                def _():
                    for a_ref, p in zip(acc_refs, accumulate([a_ref[...] for a_ref in acc_refs])):
                        a_ref[...] = p

            @pl.when(k == nk - 1)
            def _():
                finalize(accumulate([a_ref[...] for a_ref in acc_refs]))

    scratch = [pltpu.VMEM((tm, tn), F32) for _ in range(na)] if nk > 1 else []
    args = []
    for x in xs:
        args += [x] * kq
    for w in ws:
        args += [w["arr"]] * kq
    args += [e[0] for e in extras]
    outs = pl.pallas_call(
        kern, out_shape=out_shapes, grid=grid, in_specs=in_specs, out_specs=out_specs,
        scratch_shapes=scratch, name=name,
        compiler_params=_params(("parallel", "parallel", "arbitrary")),
    )(*args)
    return outs


def _ex_row(arr, tm, tn, coff=0):
    return (arr, (tm, tn), lambda i, j, k: (i, coff + j))


def _ex_col(arr, tn, coff=0):
    return (arr, (1, tn), lambda i, j, k: (0, coff + j))


class _Mod:
    def __init__(self, ada, d, per_row, row0, t_len):
        self.d, self.per_row, self.row0, self.t_len = d, per_row, row0, t_len
        rows = ada.shape[0]
        self.a2 = ada
        self.a3 = ada.reshape(rows, 1, ada.shape[1])
        self.a3r = ada.reshape(1, rows, ada.shape[1])

    def norm_operand(self, sub, kind, tt):
        slab = sub * 3 + kind
        if self.per_row:
            return self.a3r, (1, tt, self.d), (lambda b, t: (0, t, slab))
        row0 = self.row0
        return self.a3, (1, 1, self.d), (lambda b, t: (row0 + b, 0, slab))

    def mm_extra(self, sub, kind, tm, tn):
        cb = (sub * 3 + kind) * (self.d // tn)
        if self.per_row:
            return (self.a2, (tm, tn), lambda i, j, k: (i, cb + j))
        row0, t_len = self.row0, self.t_len
        return (self.a3, (1, 1, tn), lambda i, j, k: (row0 + (i * tm) // t_len, 0, cb + j))


def _adanorm(x3, g, mod, sub, out_dtype, tt, name):
    b, t, d = x3.shape
    use_mod = mod is not None

    def kern(*refs):
        if use_mod:
            x_ref, g_ref, sh_ref, sc_ref, o_ref = refs
        else:
            x_ref, g_ref, o_ref = refs
        xf = x_ref[0]
        y = xf * lax.rsqrt(jnp.mean(xf * xf, axis=-1, keepdims=True) + RMS_EPS) * g_ref[...]
        if use_mod:
            y = y * (1.0 + sc_ref[0]) + sh_ref[0]
        o_ref[0] = y.astype(o_ref.dtype)

    in_specs = [pl.BlockSpec((1, tt, d), lambda bi, ti: (bi, ti, 0)),
                pl.BlockSpec((1, d), lambda bi, ti: (0, 0))]
    args = [x3, g.reshape(1, d)]
    if use_mod:
        for kind in (0, 1):
            arr, bshape, imap = mod.norm_operand(sub, kind, tt)
            in_specs.append(pl.BlockSpec(bshape, imap))
            args.append(arr)
    return pl.pallas_call(
        kern, out_shape=jax.ShapeDtypeStruct((b, t, d), out_dtype), grid=(b, t // tt),
        in_specs=in_specs, out_specs=pl.BlockSpec((1, tt, d), lambda bi, ti: (bi, ti, 0)),
        name=name, compiler_params=_params(("parallel", "parallel")),
    )(*args)


def _rope_tables(pos, half):
    inv_freq = ROPE_THETA ** (-jnp.arange(half, dtype=F32) / half)
    ang = pos.astype(F32)[:, None] * inv_freq[None, :]
    cos, sin = jnp.cos(ang), jnp.sin(ang)
    return jnp.concatenate([cos, cos, -sin, sin], axis=-1)


def _rope_cols(w, half):
    return jnp.concatenate([w, w[..., half:], w[..., :half]], axis=-1)


def _kvprep(a, kr_raw, tab, q_norm_g, kv_norm_g, ql, kvl, rd, tm, name):
    m = a.shape[0]

    def kern(a_ref, kr_ref, tab_ref, qg_ref, kg_ref, qn_ref, ckv_ref, kro_ref, kcat_ref):
        av = a_ref[...]
        cq, ck = av[:, :ql], av[:, ql:]
        qn = cq * lax.rsqrt(jnp.mean(cq * cq, axis=-1, keepdims=True) + RMS_EPS) * qg_ref[...]
        qn_ref[...] = qn.astype(BF16)
        ckv = ck * lax.rsqrt(jnp.mean(ck * ck, axis=-1, keepdims=True) + RMS_EPS) * kg_ref[...]
        ckv_ref[...] = ckv
        r = kr_ref[...] * tab_ref[...]
        rope = r + pltpu.roll(r, rd, 1)
        kro_ref[...] = rope[:, :rd]
        lane = lax.broadcasted_iota(jnp.int32, rope.shape, 1)
        kcat_ref[...] = jnp.concatenate([ckv, jnp.where(lane < rd, rope, 0.0)], axis=1).astype(BF16)

    return pl.pallas_call(
        kern,
        out_shape=[jax.ShapeDtypeStruct((m, ql), BF16), jax.ShapeDtypeStruct((m, kvl), F32),
                   jax.ShapeDtypeStruct((m, rd), F32), jax.ShapeDtypeStruct((m, kvl + 2 * rd), BF16)],
        grid=(m // tm,),
        in_specs=[pl.BlockSpec((tm, ql + kvl), lambda i: (i, 0)),
                  pl.BlockSpec((tm, 2 * rd), lambda i: (i, 0)),
                  pl.BlockSpec((tm, 2 * rd), lambda i: (i, 0)),
                  pl.BlockSpec((1, ql), lambda i: (0, 0)),
                  pl.BlockSpec((1, kvl), lambda i: (0, 0))],
        out_specs=[pl.BlockSpec((tm, ql), lambda i: (i, 0)), pl.BlockSpec((tm, kvl), lambda i: (i, 0)),
                   pl.BlockSpec((tm, rd), lambda i: (i, 0)), pl.BlockSpec((tm, kvl + 2 * rd), lambda i: (i, 0))],
        name=name, compiler_params=_params(("parallel",)),
    )(a, kr_raw, tab, q_norm_g.reshape(1, ql), kv_norm_g.reshape(1, kvl))


def _qproj(qn, w_q, w_uk_t, tab, tm, name):
    m, ql = qn.shape
    nh, nd, kvl = w_uk_t.shape
    rd2 = w_q.shape[2] - nd

    def kern(qn_ref, wq_ref, wuk_ref, tab_ref, o_ref):
        qq = _dot(qn_ref[...], wq_ref[...].astype(BF16))
        q_lat = _dot(qq[:, :nd].astype(BF16), wuk_ref[...].astype(BF16))
        r = qq[:, nd:] * tab_ref[...]
        q_rope = r + pltpu.roll(r, rd2 // 2, 1)
        o_ref[...] = jnp.concatenate([q_lat, q_rope], axis=1).astype(BF16)

    return pl.pallas_call(
        kern, out_shape=jax.ShapeDtypeStruct((nh, m, kvl + rd2), BF16), grid=(nh, m // tm),
        in_specs=[pl.BlockSpec((tm, ql), lambda h, i: (i, 0)),
                  pl.BlockSpec((None, ql, nd + rd2), lambda h, i: (h, 0, 0)),
                  pl.BlockSpec((None, nd, kvl), lambda h, i: (h, 0, 0)),
                  pl.BlockSpec((tm, rd2), lambda h, i: (i, 0))],
        out_specs=pl.BlockSpec((None, tm, kvl + rd2), lambda h, i: (h, i, 0)),
        name=name, compiler_params=_params(("parallel", "parallel")),
    )(qn, w_q, w_uk_t, tab)


def _prompt_attention(qcat, kcat, w_uv, n_seq, t_len, scale, tq, hb, name):
    nh, m, e = qcat.shape
    _, kvl, vd = w_uv.shape
    nq = t_len // tq
    heads = range(hb)

    def kern(q_ref, k_ref, w_ref, o_ref, m_ref, l_ref, acc_ref):
        qi = pl.program_id(2)
        m_ref[...] = jnp.full(m_ref.shape, -jnp.inf, F32)
        l_ref[...] = jnp.zeros(l_ref.shape, F32)
        acc_ref[...] = jnp.zeros(acc_ref.shape, F32)

        def update(j, masked):
            kb = k_ref[pl.ds(pl.multiple_of(j * tq, tq), tq), :]
            ss = [_dot_nt(q_ref[h], kb) * scale for h in heads]
            if masked:
                row = lax.broadcasted_iota(jnp.int32, (tq, tq), 0)
                col = lax.broadcasted_iota(jnp.int32, (tq, tq), 1)
                ss = [jnp.where(col <= row, s, -jnp.inf) for s in ss]
            ps, alphas = [], []
            for h in heads:
                m_prev = m_ref[h]
                m_new = jnp.maximum(m_prev, jnp.max(ss[h], axis=-1, keepdims=True))
                alpha = jnp.exp(m_prev - m_new)
                p = jnp.exp(ss[h] - m_new)
                l_ref[h] = l_ref[h] * alpha + jnp.sum(p, axis=-1, keepdims=True)
                m_ref[h] = m_new
                ps.append(p.astype(BF16))
                alphas.append(alpha)
            pvs = [_dot(ps[h], kb[:, :kvl]) for h in heads]
            for h in heads:
                acc_ref[h] = acc_ref[h] * alphas[h] + pvs[h]

        def body(j, carry):
            update(j, False)
            return carry

        lax.fori_loop(0, qi, body, 0)
        update(qi, True)
        for h in heads:
            o_lat = (acc_ref[h] / l_ref[h]).astype(BF16)
            o_ref[:, h * vd:(h + 1) * vd] = _dot(o_lat, w_ref[h].astype(BF16)).astype(BF16)

    return pl.pallas_call(
        kern, out_shape=jax.ShapeDtypeStruct((m, nh * vd), BF16), grid=(n_seq, nh // hb, nq),
        in_specs=[pl.BlockSpec((hb, tq, e), lambda b, hg, qi: (hg, b * nq + qi, 0)),
                  pl.BlockSpec((t_len, e), lambda b, hg, qi: (b, 0)),
                  pl.BlockSpec((hb, kvl, vd), lambda b, hg, qi: (hg, 0, 0))],
        out_specs=pl.BlockSpec((tq, hb * vd), lambda b, hg, qi: (b * nq + qi, hg)),
        scratch_shapes=[pltpu.VMEM((hb, tq, 1), F32), pltpu.VMEM((hb, tq, 1), F32),
                        pltpu.VMEM((hb, tq, kvl), F32)],
        name=name, compiler_params=_params(("parallel", "parallel", "arbitrary")),
    )(qcat, kcat, w_uv)


def _decode_attention(q, cache_kv, cache_kr_t, layer, page_table, ckv_new, kr_new, scale, pp, name):
    bsz, nh, e = q.shape
    page, kvl = cache_kv.shape[2], cache_kv.shape[3]
    rd = cache_kr_t.shape[2]
    n_pages = page_table.shape[1]
    assert n_pages % pp == 0
    ns = n_pages // pp

    def kern(pt_ref, q_ref, *refs):
        kv_refs = refs[:pp]
        kr_refs = refs[pp:2 * pp]
        cn_ref, rn_ref, o_ref, m_ref, l_ref, acc_ref, kvb_ref, krb_ref = refs[2 * pp:]
        s_idx = pl.program_id(1)

        @pl.when(s_idx == 0)
        def _():
            m_ref[...] = jnp.full(m_ref.shape, -jnp.inf, F32)
            l_ref[...] = jnp.zeros(l_ref.shape, F32)
            acc_ref[...] = jnp.zeros(acc_ref.shape, F32)

        qv = q_ref[0]
        ql, qr = qv[:, :kvl], qv[:, kvl:kvl + rd]

        def online(s, pv_fn):
            m_prev = m_ref[...]
            m_new = jnp.maximum(m_prev, jnp.max(s, axis=-1, keepdims=True))
            alpha = jnp.exp(m_prev - m_new)
            p = jnp.exp(s - m_new)
            l_ref[...] = l_ref[...] * alpha + jnp.sum(p, axis=-1, keepdims=True)
            acc_ref[...] = acc_ref[...] * alpha + pv_fn(p)
            m_ref[...] = m_new

        for j in range(pp):
            kvb_ref[j * page:(j + 1) * page, :] = kv_refs[j][...].astype(BF16)
            krb_ref[:, j * page:(j + 1) * page] = kr_refs[j][...].astype(BF16)
        kv_all = kvb_ref[...]
        s = (_dot_nt(ql, kv_all) + _dot(qr, krb_ref[...])) * scale
        online(s, lambda p: _dot(p.astype(BF16), kv_all))

        @pl.when(s_idx == ns - 1)
        def _():
            cn = cn_ref[0].astype(BF16).astype(F32)
            rn = rn_ref[0].astype(BF16).astype(F32)
            s_new = (jnp.sum(ql.astype(F32) * cn, axis=-1, keepdims=True)
                     + jnp.sum(qr.astype(F32) * rn, axis=-1, keepdims=True)) * scale
            online(s_new, lambda p: p.astype(BF16).astype(F32) * cn)
            o_ref[0] = acc_ref[...] / l_ref[...]

    in_specs = [pl.BlockSpec((1, nh, e), lambda b, s, pt: (b, 0, 0))]
    for j in range(pp):
        in_specs.append(pl.BlockSpec((None, None, page, kvl),
                                     lambda b, s, pt, j=j: (layer, pt[b, s * pp + j], 0, 0)))
    for j in range(pp):
        in_specs.append(pl.BlockSpec((None, None, rd, page),
                                     lambda b, s, pt, j=j: (layer, pt[b, s * pp + j], 0, 0)))
    in_specs.append(pl.BlockSpec((1, 1, kvl), lambda b, s, pt: (b, 0, 0)))
    in_specs.append(pl.BlockSpec((1, 1, rd), lambda b, s, pt: (b, 0, 0)))
    grid_spec = pltpu.PrefetchScalarGridSpec(
        num_scalar_prefetch=1, grid=(bsz, ns), in_specs=in_specs,
        out_specs=pl.BlockSpec((1, nh, kvl), lambda b, s, pt: (b, 0, 0)),
        scratch_shapes=[pltpu.VMEM((nh, 1), F32), pltpu.VMEM((nh, 1), F32), pltpu.VMEM((nh, kvl), F32),
                        pltpu.VMEM((pp * page, kvl), BF16), pltpu.VMEM((rd, pp * page), BF16)])
    return pl.pallas_call(
        kern, out_shape=jax.ShapeDtypeStruct((bsz, nh, kvl), F32), grid_spec=grid_spec, name=name,
        compiler_params=_params(("parallel", "arbitrary")),
    )(page_table, q, *([cache_kv] * pp), *([cache_kr_t] * pp), ckv_new, kr_new)


def _oproj(o_lat, w_uv, tm, name):
    nh, m, kvl = o_lat.shape
    vd = w_uv.shape[2]

    def kern(o_ref, w_ref, y_ref):
        y_ref[...] = _dot(o_ref[...].astype(BF16), w_ref[...].astype(BF16)).astype(BF16)

    return pl.pallas_call(
        kern, out_shape=jax.ShapeDtypeStruct((m, nh * vd), BF16), grid=(nh, m // tm),
        in_specs=[pl.BlockSpec((None, tm, kvl), lambda h, i: (h, i, 0)),
                  pl.BlockSpec((None, kvl, vd), lambda h, i: (h, 0, 0))],
        out_specs=pl.BlockSpec((tm, vd), lambda h, i: (i, h)),
        name=name, compiler_params=_params(("parallel", "parallel")),
    )(o_lat, w_uv)


def _conv_prompt(bcu, conv_w, n_seq, t_len, cd, tc, name):
    m = bcu.shape[0]
    ncb = cd // tc
    cw = conv_w.shape[0]

    def kern(b_ref, c_ref, u_ref, w_ref, y_ref, st_ref):
        xg = c_ref[...] * u_ref[...]
        w = w_ref[...]
        row = lax.broadcasted_iota(jnp.int32, xg.shape, 0)
        y = xg * w[cw - 1:cw]
        for d in range(1, cw):
            sh = jnp.where(row >= d, pltpu.roll(xg, d, 0), 0.0)
            y = y + sh * w[cw - 1 - d:cw - d]
        y_ref[...] = (b_ref[...] * y).astype(BF16)
        st_ref[0] = xg[t_len - (cw - 1):, :]

    return pl.pallas_call(
        kern, out_shape=[jax.ShapeDtypeStruct((m, cd), BF16), jax.ShapeDtypeStruct((n_seq, cw - 1, cd), F32)],
        grid=(n_seq, ncb),
        in_specs=[pl.BlockSpec((t_len, tc), lambda b, j: (b, j)),
                  pl.BlockSpec((t_len, tc), lambda b, j: (b, ncb + j)),
                  pl.BlockSpec((t_len, tc), lambda b, j: (b, 2 * ncb + j)),
                  pl.BlockSpec((cw, tc), lambda b, j: (0, j))],
        out_specs=[pl.BlockSpec((t_len, tc), lambda b, j: (b, j)),
                   pl.BlockSpec((1, cw - 1, tc), lambda b, j: (b, 0, j))],
        name=name, compiler_params=_params(("parallel", "parallel")),
    )(bcu, bcu, bcu, conv_w)


def _conv_step(bcu, prev, conv_w, cd, tc, name):
    bsz = bcu.shape[0]
    ncb = cd // tc
    cw = conv_w.shape[0]

    def kern(*refs):
        b_ref, c_ref, u_ref, w_ref = refs[:4]
        p_refs = refs[4:4 + cw - 1]
        y_ref = refs[4 + cw - 1]
        st_refs = refs[5 + cw - 1:]
        xg = c_ref[...] * u_ref[...]
        w = w_ref[...]
        y = xg * w[cw - 1:cw]
        for d in range(cw - 1):
            y = y + p_refs[d][...] * w[d:d + 1]
        y_ref[...] = (b_ref[...] * y).astype(BF16)
        for d in range(cw - 2):
            st_refs[d][...] = p_refs[d + 1][...]
        st_refs[cw - 2][...] = xg

    in_specs = [pl.BlockSpec((bsz, tc), lambda j: (0, j)),
                pl.BlockSpec((bsz, tc), lambda j: (0, ncb + j)),
                pl.BlockSpec((bsz, tc), lambda j: (0, 2 * ncb + j)),
                pl.BlockSpec((cw, tc), lambda j: (0, j))]
    for d in range(cw - 1):
        in_specs.append(pl.BlockSpec((bsz, tc), lambda j, d=d: (0, d * ncb + j)))
    outs = pl.pallas_call(
        kern,
        out_shape=[jax.ShapeDtypeStruct((bsz, cd), BF16)] + [jax.ShapeDtypeStruct((bsz, cd), F32)] * (cw - 1),
        grid=(ncb,), in_specs=in_specs,
        out_specs=[pl.BlockSpec((bsz, tc), lambda j: (0, j))] * cw,
        name=name, compiler_params=_params(("parallel",)),
    )(bcu, bcu, bcu, conv_w, *([prev] * (cw - 1)))
    return outs[0], jnp.stack(outs[1:], axis=1)


def _token_mix(h3, prev3, mu, tc, name):
    b, t, d = h3.shape
    nmix = mu.shape[0]
    shift_here = prev3 is None

    def kern(*refs):
        if shift_here:
            h_ref, mu_ref = refs[:2]
            o_refs = refs[2:]
        else:
            h_ref, p_ref, mu_ref = refs[:3]
            o_refs = refs[3:]
        hh = h_ref[0]
        if shift_here:
            row = lax.broadcasted_iota(jnp.int32, hh.shape, 0)
            prev = jnp.where(row >= 1, pltpu.roll(hh, 1, 0), 0.0)
        else:
            prev = p_ref[0]
        xx = prev - hh
        mv = mu_ref[...]
        for i in range(nmix):
            o_refs[i][0] = (hh + xx * mv[i:i + 1]).astype(BF16)

    blk = pl.BlockSpec((1, t, tc), lambda bi, j: (bi, 0, j))
    in_specs = [blk] + ([] if shift_here else [blk]) + [pl.BlockSpec((nmix, tc), lambda bi, j: (0, j))]
    args = [h3] + ([] if shift_here else [prev3]) + [mu]
    return pl.pallas_call(
        kern, out_shape=[jax.ShapeDtypeStruct((b, t, d), BF16)] * nmix, grid=(b, d // tc),
        in_specs=in_specs, out_specs=[blk] * nmix,
        name=name, compiler_params=_params(("parallel", "parallel")),
    )(*args)


def _rowsum_mxu(xs, ones):
    his = [x.astype(BF16) for x in xs]
    los = [(x - hi.astype(F32)).astype(BF16) for x, hi in zip(xs, his)]
    return [_dot(hi, ones) + _dot(lo, ones) for hi, lo in zip(his, los)]


def _wkv_prompt(r, k, v, a, lw, g, k_k, k_a, r_k, ln_w, ln_b, n_seq, t_len, hs, name):
    m, d = r.shape
    c = WKV_CHUNK
    hl = min(WKV_LANES, d)
    hpb = hl // hs
    nh = d // hs
    nc = t_len // c

    def kern(r_ref, k_ref, v_ref, a_ref, lw_ref, g_ref, kk_ref, ka_ref, rk_ref, lnw_ref, lnb_ref,
             o_ref, s_ref):
        ci = pl.program_id(2)

        @pl.when(ci == 0)
        def _():
            s_ref[...] = jnp.zeros(s_ref.shape, F32)

        ri = lax.broadcasted_iota(jnp.int32, (c, c), 0)
        cj = lax.broadcasted_iota(jnp.int32, (c, c), 1)
        strict = cj < ri
        ri2 = lax.broadcasted_iota(jnp.int32, (c, 2 * c), 0)
        cj2 = lax.broadcasted_iota(jnp.int32, (c, 2 * c), 1)
        incl2 = jnp.where(cj2 >= c, cj2 - c, cj2) <= ri2
        tri = jnp.where(cj <= ri, 1.0, 0.0).astype(BF16)
        eye = jnp.where(cj == ri, 1.0, 0.0).astype(F32)
        heads = range(hpb)
        sls = [slice(hh * hs, (hh + 1) * hs) for hh in heads]

        lw_all = lw_ref[...]
        lw_hi = lw_all.astype(BF16)
        lw_lo = (lw_all - lw_hi.astype(F32)).astype(BF16)
        cum_all = _dot(tri, lw_hi) + _dot(tri, lw_lo)
        ones = jnp.ones((hs, hs), BF16)
        kk0 = [k_ref[:, sl] * kk_ref[:, sl] for sl in sls]
        ssq = _rowsum_mxu([x * x for x in kk0], ones)
        at, rt, bk_t, bk_h, vb, kmods, gtot = [], [], [], [], [], [], []
        for hh in heads:
            sl = sls[hh]
            rr, k0, aa, lwh = r_ref[:, sl], k_ref[:, sl], a_ref[:, sl], lw_all[:, sl]
            kk = kk0[hh] * lax.rsqrt(jnp.maximum(ssq[hh], 1e-24))
            kmod = k0 * (1.0 + (aa - 1.0) * ka_ref[:, sl])
            avec, bvec = -kk, kk * aa
            cum = cum_all[:, sl]
            tot = cum[c - 1:c, :]
            at.append((avec * jnp.exp(cum - lwh)).astype(BF16))
            rt.append((rr * jnp.exp(cum)).astype(BF16))
            e_neg = jnp.exp(-cum)
            e_tot = jnp.exp(tot - cum)
            bk_t.append(jnp.concatenate([bvec * e_neg, kmod * e_neg], axis=0).astype(BF16))
            bk_h.append(jnp.concatenate([bvec * e_tot, kmod * e_tot], axis=0).astype(BF16))
            vb.append(v_ref[:, sl].astype(BF16))
            kmods.append(kmod)
            gtot.append(jnp.exp(tot))
        a_ab = [jnp.where(strict, _dot_nt(at[hh], bk_t[hh][:c]), 0.0) for hh in heads]
        a_ak = [jnp.where(strict, _dot_nt(at[hh], bk_t[hh][c:]), 0.0).astype(BF16) for hh in heads]
        a_r = [jnp.where(incl2, _dot_nt(rt[hh], bk_t[hh]), 0.0).astype(BF16) for hh in heads]
        s0 = [s_ref[0, hh] for hh in heads]
        ws = [_dot_nt(jnp.concatenate([at[hh], rt[hh]], axis=0), s0[hh].astype(BF16)) for hh in heads]
        akv = [_dot(a_ak[hh], vb[hh]) for hh in heads]
        tinv = [eye + a_ab[hh] for hh in heads]
        lp = [a_ab[hh].astype(BF16) for hh in heads]
        n_done = 2
        while n_done < c:
            lp = [_dot(lp[hh], lp[hh]).astype(BF16) for hh in heads]
            tinv = [tinv[hh] + _dot(tinv[hh].astype(BF16), lp[hh]) for hh in heads]
            n_done *= 2
        p = [_dot(tinv[hh].astype(BF16), (ws[hh][:c] + akv[hh]).astype(BF16)) for hh in heads]
        z = [jnp.concatenate([p[hh].astype(BF16), vb[hh]], axis=0) for hh in heads]
        y = [ws[hh][c:] + _dot(a_r[hh], z[hh]) for hh in heads]
        s1 = [s0[hh] * gtot[hh] + _dot_tn(z[hh], bk_h[hh]) for hh in heads]
        rs = _rowsum_mxu([jnp.concatenate([y[hh], y[hh] * y[hh], r_ref[:, sls[hh]] * kmods[hh] * rk_ref[:, sls[hh]]],
                                          axis=0) for hh in heads], ones)
        for hh in heads:
            sl = sls[hh]
            s_ref[0, hh] = s1[hh]
            mean = rs[hh][:c] * (1.0 / hs)
            var = jnp.maximum(rs[hh][c:2 * c] * (1.0 / hs) - mean * mean, 0.0)
            yn = (y[hh] - mean) * lax.rsqrt(var + GN_EPS) * lnw_ref[:, sl] + lnb_ref[:, sl]
            o_ref[:, sl] = ((yn + rs[hh][2 * c:] * v_ref[:, sl]) * g_ref[:, sl]).astype(BF16)

    blk = pl.BlockSpec((c, hl), lambda b, hg, ci: (b * nc + ci, hg))
    wblk = pl.BlockSpec((1, hl), lambda b, hg, ci: (0, hg))
    return pl.pallas_call(
        kern, out_shape=[jax.ShapeDtypeStruct((m, d), BF16), jax.ShapeDtypeStruct((n_seq, nh, hs, hs), F32)],
        grid=(n_seq, d // hl, nc),
        in_specs=[blk] * 6 + [wblk] * 5,
        out_specs=[blk, pl.BlockSpec((1, hpb, hs, hs), lambda b, hg, ci: (b, hg, 0, 0))],
        name=name, compiler_params=_params(("parallel", "parallel", "arbitrary")),
    )(r, k, v, a, lw, g, k_k, k_a, r_k, ln_w, ln_b)


def _wkv_step(r, k, a, lw, v, g, s0, k_k, k_a, r_k, ln_w, ln_b, hb, name):
    nh, hs, bsz = r.shape

    def kern(r_ref, k_ref, a_ref, lw_ref, v_ref, g_ref, s_ref, kk_ref, ka_ref, rk_ref, lnw_ref, lnb_ref,
             o_ref, so_ref):
        rr, k0, aa = r_ref[...], k_ref[...], a_ref[...]
        vv = v_ref[...]
        dec = jnp.exp(lw_ref[...])
        kk = k0 * kk_ref[...]
        kk = kk * lax.rsqrt(jnp.maximum(jnp.sum(kk * kk, axis=1, keepdims=True), 1e-24))
        kmod = k0 * (1.0 + (aa - 1.0) * ka_ref[...])
        avec, bvec = -kk, kk * aa
        s = s_ref[...]
        sa = jnp.sum(s * avec[:, None], axis=2, keepdims=True)
        s1 = s * dec[:, None] + sa * bvec[:, None] + vv * kmod[:, None]
        so_ref[...] = s1
        y = jnp.sum(s1 * rr[:, None], axis=2, keepdims=True)
        mean = jnp.mean(y, axis=1, keepdims=True)
        var = jnp.mean(jnp.square(y - mean), axis=1, keepdims=True)
        yn = (y - mean) * lax.rsqrt(var + GN_EPS) * lnw_ref[...] + lnb_ref[...]
        bonus = jnp.sum(rr * kmod * rk_ref[...], axis=1, keepdims=True)[:, None] * vv
        o_ref[...] = (yn + bonus) * g_ref[...]

    kblk = pl.BlockSpec((hb, hs, bsz), lambda i: (i, 0, 0))
    vblk = pl.BlockSpec((hb, hs, 1, bsz), lambda i: (i, 0, 0, 0))
    sblk = pl.BlockSpec((hb, hs, hs, bsz), lambda i: (i, 0, 0, 0))
    wkblk = pl.BlockSpec((hb, hs, 1), lambda i: (i, 0, 0))
    wvblk = pl.BlockSpec((hb, hs, 1, 1), lambda i: (i, 0, 0, 0))
    return pl.pallas_call(
        kern, out_shape=[jax.ShapeDtypeStruct((nh, hs, 1, bsz), F32), jax.ShapeDtypeStruct((nh, hs, hs, bsz), F32)],
        grid=(nh // hb,),
        in_specs=[kblk] * 4 + [vblk] * 2 + [sblk] + [wkblk] * 3 + [wvblk] * 2,
        out_specs=[vblk, sblk],
        name=name, compiler_params=_params(("parallel",)),
    )(r, k, a, lw, v, g, s0, k_k, k_a, r_k, ln_w, ln_b)


def _pick(n, candidates):
    for c in candidates:
        if n % c == 0:
            return c
    return n


def kernel(x_prompt, x_sample, cache_kv_latent, cache_k_rope, state_conv, state_rwkv_shift, state_rwkv_wkv, page_table, c_prompt, c_sample, w_ada, b_ada, norm_g, ffn_w1, ffn_w3, ffn_w2, mla_w_in, mla_q_norm, mla_kv_norm, mla_w_uq, mla_w_ukv, conv_w, mix_w_out, rwkv_mu, rwkv_w0, rwkv_w1, rwkv_w2, rwkv_a0, rwkv_a1, rwkv_a2, rwkv_g1, rwkv_g2, rwkv_k_k, rwkv_k_a, rwkv_r_k, rwkv_w_r, rwkv_w_k, rwkv_w_v, rwkv_w_o, rwkv_ln_w, rwkv_ln_b, final_norm_g):
    nb, seq, d = x_prompt.shape
    nsb, dec_seq, _ = x_sample.shape
    assert dec_seq == 1
    depth = w_ada.shape[0]
    dff = ffn_w1.shape[-1]
    ql = mla_q_norm.shape[1]
    kvl = mla_kv_norm.shape[1]
    rd = cache_k_rope.shape[-1]
    nh_mla = mla_w_ukv.shape[2]
    nd = mla_w_uq.shape[2] // nh_mla - rd
    vd = mla_w_ukv.shape[3] - nd
    cd = conv_w.shape[-1]
    cw = conv_w.shape[1]
    hs = state_rwkv_wkv.shape[-1]
    nh_rwkv = d // hs
    past_len = page_table.shape[1] * cache_kv_latent.shape[2]
    mla_scale = float((nd + rd) ** -0.5)
    mp = nb * seq

    n_c = nsb + nb
    n_c_pad = -(-n_c // 16) * 16
    c_all = jnp.concatenate([c_sample, c_prompt, jnp.zeros((n_c_pad - n_c, d), F32)], axis=0)
    t_ada = _pick(9 * d, (512, 256, 128))
    adas = []
    for layer in range(depth):
        (ada,) = _mm([c_all], [_w(w_ada, (layer,))], [[(0, 0)]],
                     lambda acc, ev: (acc[0] + ev[0],), [F32],
                     n=9 * d, tm=n_c_pad, tn=t_ada, x_pro=_silu,
                     extras=[_ex_col(b_ada[layer].reshape(1, 9 * d), t_ada)], name=f"ada_l{layer}")
        adas.append(ada)

    ffn_bf16 = {}

    def run_group(x2, is_sample):
        m = x2.shape[0]
        if is_sample:
            n_seq, t_len = m, 1
            xshape = (1, m, d)
            tm = m
            tt = m
            pos = jnp.full((m,), past_len, jnp.int32)
        else:
            n_seq, t_len = nb, seq
            xshape = (nb, seq, d)
            tm = _pick(seq, (1024, 512, 256, 128))
            tt = _pick(seq, (256, 128))
            pos = jnp.tile(jnp.arange(seq, dtype=jnp.int32), nb)
        tn = _pick(d, (512, 256, 128))
        tab = _rope_tables(pos, rd // 2)
        kv_rows, kr_rows, conv_states, shift_states, wkv_states = [], [], [], [], []

        def ffn(x2, mod, layer, idx, sub, tag):
            h = _adanorm(x2.reshape(xshape), norm_g[layer, sub], mod, sub, BF16, tt, f"norm_{tag}").reshape(m, d)
            tnf = _pick(dff, (256, 128))
            up_epi = lambda acc, ev: (_silu(acc[0]) * acc[1],)
            down_epi = lambda acc, ev: (ev[0] + FFN_RES * ev[1] * acc[0],)
            if is_sample:
                gact, w1b, w3b = _mm([h], [_w(ffn_w1, (layer, idx)), _w(ffn_w3, (layer, idx))],
                                     [[(0, 0)], [(0, 1)]], up_epi, [BF16],
                                     n=dff, tm=tm, tn=tnf, cast_ws=(0, 1), name=f"ffn_up_{tag}")
                tnd = _pick(d, (256, 128))
                xo, w2b = _mm([gact], [_w(ffn_w2, (layer, idx))], [[(0, 0)]], down_epi, [F32],
                              n=d, tm=tm, tn=tnd, cast_ws=(0,),
                              extras=[_ex_row(x2, tm, tnd), mod.mm_extra(sub, 2, tm, tnd)], name=f"ffn_down_{tag}")
                ffn_bf16[(layer, idx)] = (w1b, w3b, w2b)
            else:
                w1b, w3b, w2b = ffn_bf16[(layer, idx)]
                (gact,) = _mm([h], [_w(w1b), _w(w3b)], [[(0, 0)], [(0, 1)]], up_epi, [BF16],
                              n=dff, tm=tm, tn=tnf, name=f"ffn_up_{tag}")
                tnd = _pick(d, (1024, 512, 256, 128))
                (xo,) = _mm([gact], [_w(w2b)], [[(0, 0)]], down_epi, [F32],
                            n=d, tm=tm, tn=tnd, tk=_pick(dff, (256, 128)), kq=4,
                            extras=[_ex_row(x2, tm, tnd), mod.mm_extra(sub, 2, tm, tnd)], name=f"ffn_down_{tag}")
            return xo

        for layer in range(depth):
            tag = f"{'s' if is_sample else 'p'}{layer}"
            mod = _Mod(adas[layer], d, is_sample, nsb, t_len)
            x2 = ffn(x2, mod, layer, 0, 0, tag + "a")
            if layer % 2 == 0:
                i = layer // 2
                h = _adanorm(x2.reshape(xshape), norm_g[layer, 1], mod, 1, BF16, tt, f"norm_mix_{tag}").reshape(m, d)
                w_in = mla_w_in[i]
                n_a = ql + kvl
                (a_proj,) = _mm([h], [_w(w_in)], [[(0, 0)]], lambda acc, ev: (acc[0],), [F32],
                                n=n_a, tm=tm, tn=_pick(n_a, (512, 256, 128)), name=f"mla_in_a_{tag}")
                w_kr = _rope_cols(w_in[:, n_a:n_a + rd], rd // 2)
                (kr_raw,) = _mm([h], [_w(w_kr)], [[(0, 0)]], lambda acc, ev: (acc[0],), [F32],
                                n=2 * rd, tm=tm, tn=2 * rd, name=f"mla_in_kr_{tag}")
                w_bcu = w_in[:, n_a + rd:]
                (bcu,) = _mm([h], [_w(w_bcu)], [[(0, 0)]], lambda acc, ev: (acc[0],), [F32],
                             n=3 * cd, tm=tm, tn=_pick(cd, (512, 256, 128)), name=f"mla_in_bcu_{tag}")
                qn, ckv, kr, kcat = _kvprep(a_proj, kr_raw, tab, mla_q_norm[i], mla_kv_norm[i], ql, kvl, rd,
                                            _pick(m, (256, 128)), f"kvprep_{tag}")
                w_uq = mla_w_uq[i].reshape(ql, nh_mla, nd + rd)
                w_uq_h = jnp.transpose(w_uq, (1, 0, 2))
                w_q = jnp.concatenate([w_uq_h[:, :, :nd], _rope_cols(w_uq_h[:, :, nd:], rd // 2)], axis=-1)
                w_uk_t = jnp.transpose(mla_w_ukv[i][:, :, :nd], (1, 2, 0))
                w_uv = jnp.transpose(mla_w_ukv[i][:, :, nd:], (1, 0, 2))
                qcat = _qproj(qn, w_q, w_uk_t, tab, _pick(m, (1024, 512, 256, 128)), f"qproj_{tag}")
                if is_sample:
                    o_lat = _decode_attention(jnp.transpose(qcat, (1, 0, 2)), cache_kv_latent,
                                              jnp.swapaxes(cache_k_rope, 2, 3), i,
                                              page_table, ckv.reshape(m, 1, kvl), kr.reshape(m, 1, rd),
                                              mla_scale, _pick(page_table.shape[1], (16, 8, 4, 2, 1)), f"attn_{tag}")
                    o_att = _oproj(jnp.transpose(o_lat, (1, 0, 2)), w_uv, _pick(m, (512, 256, 128)), f"oproj_{tag}")
                    prev = state_conv[i].reshape(m, (cw - 1) * cd)
                    conv_y, conv_new = _conv_step(bcu, prev, conv_w[i], cd, _pick(cd, (512, 256, 128)), f"conv_{tag}")
                else:
                    o_att = _prompt_attention(qcat, kcat, w_uv, n_seq, t_len, mla_scale,
                                              _pick(t_len, (ATTN_TQ, 128)), _pick(nh_mla, (ATTN_HEADS, 2, 1)),
                                              f"attn_{tag}")
                    conv_y, conv_new = _conv_prompt(bcu, conv_w[i], n_seq, t_len, cd, _pick(cd, (256, 128)),
                                                    f"conv_{tag}")
                assert nh_mla * vd == cd
                (x2,) = _mm([o_att, conv_y], [_w(mix_w_out, (i,)), _w(mix_w_out, (i,), roff=1)], [[(0, 0), (1, 1)]],
                            lambda acc, ev: (ev[0] + ev[1] * acc[0],), [F32],
                            n=d, tm=tm, tn=tn, extras=[_ex_row(x2, tm, tn), mod.mm_extra(1, 2, tm, tn)],
                            name=f"mix_out_{tag}")
                kv_rows.append(ckv.reshape(n_seq, t_len, kvl))
                kr_rows.append(kr.reshape(n_seq, t_len, rd))
                conv_states.append(conv_new)
            else:
                j = layer // 2
                h3 = _adanorm(x2.reshape(xshape), norm_g[layer, 1], mod, 1, F32, tt, f"norm_mix_{tag}")
                tcm = _pick(d, (256, 128))
                if is_sample:
                    mixes = _token_mix(h3, state_rwkv_shift[j].reshape(xshape), rwkv_mu[j], tcm, f"tmix_{tag}")
                    shift_states.append(h3.reshape(m, d))
                else:
                    mixes = _token_mix(h3, None, rwkv_mu[j], tcm, f"tmix_{tag}")
                    shift_states.append(h3[:, -1])
                xr, xw, xk, xv, xa, xg = (t.reshape(m, d) for t in mixes)

                def proj(x, w, tag2):
                    (o,) = _mm([x], [_w(w, (j,))], [[(0, 0)]], lambda acc, ev: (acc[0],), [F32],
                               n=d, tm=tm, tn=tn, name=f"rwkv_{tag2}_{tag}")
                    return o

                r, k, v = proj(xr, rwkv_w_r, "r"), proj(xk, rwkv_w_k, "k"), proj(xv, rwkv_w_v, "v")
                dl = rwkv_w1.shape[-1]
                (t1,) = _mm([xw], [_w(rwkv_w1, (j,))], [[(0, 0)]], lambda acc, ev: (jnp.tanh(acc[0]),), [BF16],
                            n=dl, tm=tm, tn=dl, name=f"rwkv_w1_{tag}")

                def decay_epi(acc, ev):
                    z = -(ev[0] + acc[0])
                    softplus = jnp.maximum(z, 0.0) + jnp.log(1.0 + jnp.exp(-jnp.abs(z)))
                    return (-jnp.exp(-softplus - 0.5),)

                (lw,) = _mm([t1], [_w(rwkv_w2, (j,))], [[(0, 0)]], decay_epi, [F32],
                            n=d, tm=tm, tn=tn, extras=[_ex_col(rwkv_w0[j].reshape(1, d), tn)], name=f"rwkv_w2_{tag}")
                al = rwkv_a1.shape[-1]
                (t2,) = _mm([xa], [_w(rwkv_a1, (j,))], [[(0, 0)]], lambda acc, ev: (acc[0],), [BF16],
                            n=al, tm=tm, tn=al, name=f"rwkv_a1_{tag}")
                (a,) = _mm([t2], [_w(rwkv_a2, (j,))], [[(0, 0)]], lambda acc, ev: (_sigmoid(ev[0] + acc[0]),), [F32],
                           n=d, tm=tm, tn=tn, extras=[_ex_col(rwkv_a0[j].reshape(1, d), tn)], name=f"rwkv_a2_{tag}")
                gl = rwkv_g1.shape[-1]
                glp = -(-gl // 128) * 128
                g1p = jnp.pad(rwkv_g1[j], ((0, 0), (0, glp - gl)))
                g2p = jnp.pad(rwkv_g2[j], ((0, glp - gl), (0, 0)))
                (t3,) = _mm([xg], [_w(g1p)], [[(0, 0)]], lambda acc, ev: (_sigmoid(acc[0]),), [BF16],
                            n=glp, tm=tm, tn=glp, name=f"rwkv_g1_{tag}")
                (g,) = _mm([t3], [_w(g2p)], [[(0, 0)]], lambda acc, ev: (acc[0],), [F32],
                           n=d, tm=tm, tn=tn, name=f"rwkv_g2_{tag}")
                if is_sample:
                    tk3 = lambda t: t.T.reshape(nh_rwkv, hs, m)
                    tv4 = lambda t: t.T.reshape(nh_rwkv, hs, 1, m)
                    wk3 = lambda t: t.reshape(nh_rwkv, hs, 1)
                    wv4 = lambda t: t.reshape(nh_rwkv, hs, 1, 1)
                    o4, s_t = _wkv_step(tk3(r), tk3(k), tk3(a), tk3(lw), tv4(v), tv4(g),
                                        jnp.transpose(state_rwkv_wkv[j], (1, 2, 3, 0)),
                                        wk3(rwkv_k_k[j]), wk3(rwkv_k_a[j]), wk3(rwkv_r_k[j]),
                                        wv4(rwkv_ln_w[j]), wv4(rwkv_ln_b[j]),
                                        _pick(nh_rwkv, (2, 1)), f"wkv_{tag}")
                    o_pre = o4.reshape(d, m).T
                    s_new = jnp.transpose(s_t, (3, 0, 1, 2))
                else:
                    w2d = lambda t: t.reshape(1, d)
                    o_pre, s_new = _wkv_prompt(r, k, v, a, lw, g, w2d(rwkv_k_k[j]), w2d(rwkv_k_a[j]),
                                               w2d(rwkv_r_k[j]), w2d(rwkv_ln_w[j]), w2d(rwkv_ln_b[j]),
                                               n_seq, t_len, hs, f"wkv_{tag}")
                (x2,) = _mm([o_pre], [_w(rwkv_w_o, (j,))], [[(0, 0)]],
                            lambda acc, ev: (ev[0] + ev[1] * acc[0],), [F32],
                            n=d, tm=tm, tn=tn, extras=[_ex_row(x2, tm, tn), mod.mm_extra(1, 2, tm, tn)],
                            name=f"rwkv_o_{tag}")
                wkv_states.append(s_new)
            x2 = ffn(x2, mod, layer, 1, 2, tag + "b")
        y = _adanorm(x2.reshape(xshape), final_norm_g, None, 0, F32, tt, f"final_norm_{'s' if is_sample else 'p'}")
        return (y.reshape(n_seq, t_len, d), jnp.stack(kv_rows), jnp.stack(kr_rows), jnp.stack(conv_states),
                jnp.stack(shift_states), jnp.stack(wkv_states))

    y_s, s_kv, s_kr, s_conv, s_shift, s_wkv = run_group(x_sample.reshape(nsb, d), True)
    y_p, p_kv, p_kr, p_conv, p_shift, p_wkv = run_group(x_prompt.reshape(mp, d), False)
    return (y_p, y_s, p_kv, p_kr, p_conv, p_shift, p_wkv, s_kv, s_kr, s_conv, s_shift, s_wkv)
```

```python
import functools

import jax
import jax.numpy as jnp
from jax import lax
from jax.experimental import pallas as pl
from jax.experimental.pallas import tpu as pltpu

F32 = jnp.float32
BF16 = jnp.bfloat16

RMS_EPS = 1e-6
GN_EPS = 64e-5
ROPE_THETA = 10000.0
FFN_RES = 0.5
WKV_CHUNK = 64
WKV_LANES = 1024
ATTN_TQ = 512
ATTN_HEADS = 4
V7X_VMEM_LIMIT_BYTES = 56 * 1024 * 1024
HI = lax.Precision.HIGHEST


def _params(sem):
    return pltpu.CompilerParams(dimension_semantics=sem, vmem_limit_bytes=V7X_VMEM_LIMIT_BYTES)


def _dot(a, b, precision=None):
    return jnp.dot(a, b, preferred_element_type=F32, precision=precision)


def _dot_nt(a, b, precision=None):
    return lax.dot_general(a, b, (((1,), (1,)), ((), ())), preferred_element_type=F32, precision=precision)


def _dot_tn(a, b, precision=None):
    return lax.dot_general(a, b, (((0,), (0,)), ((), ())), preferred_element_type=F32, precision=precision)


def _sigmoid(x):
    return 1.0 / (1.0 + jnp.exp(-x))


def _silu(x):
    return x * _sigmoid(x)


def _w(arr, lead=(), roff=0, coff=0):
    return dict(arr=arr, lead=tuple(lead), roff=roff, coff=coff)


def _mm(xs, ws, accs, epi, out_dtypes, *, n, tm, tn, tk=None, kq=1, extras=(), x_pro=None, cast_ws=(),
        x_resident=False, ksplit=1, name):
    m, kdim = xs[0].shape
    tk = kdim if tk is None else tk
    assert m % tm == 0 and kdim % tk == 0, (name, m, tm, kdim, tk)
    assert not cast_ws or (m == tm and kq == 1)
    nchunk = kdim // tk
    nk = pl.cdiv(nchunk, kq)
    ragged = nk * kq != nchunk
    grid = (m // tm, pl.cdiv(n, tn), nk)
    nx, nw, ne, na, no = len(xs), len(ws), len(extras), len(accs), len(out_dtypes)

    def chunk(k, p):
        c = k * kq + p
        return jnp.minimum(c, nchunk - 1) if ragged else c

    in_specs = []
    for _ in xs:
        for p in range(kq):
            mode = dict(pipeline_mode=pl.Buffered(1)) if x_resident else {}
            in_specs.append(pl.BlockSpec((tm, tk), lambda i, j, k, p=p: (i, chunk(k, p)), **mode))
    for w in ws:
        nlead = len(w["lead"])
        assert w["arr"].ndim == nlead + 2
        for p in range(kq):
            def w_map(i, j, k, w=w, p=p):
                return w["lead"] + (w["roff"] + chunk(k, p), w["coff"] + j)

            in_specs.append(pl.BlockSpec((None,) * nlead + (tk, tn), w_map))
    for (_, bshape, imap) in extras:
        in_specs.append(pl.BlockSpec(bshape, imap))
    out_shapes = [jax.ShapeDtypeStruct((m, n), dt) for dt in out_dtypes]
    out_specs = [pl.BlockSpec((tm, tn), lambda i, j, k: (i, j)) for _ in out_dtypes]
    for wi in cast_ws:
        out_shapes.append(jax.ShapeDtypeStruct(ws[wi]["arr"].shape[-2:], BF16))
        out_specs.append(pl.BlockSpec((tk, tn), lambda i, j, k: (k, j)))
    nc = len(cast_ws)

    def kern(*refs):
        x_refs = refs[:nx * kq]
        w_refs = refs[nx * kq:(nx + nw) * kq]
        e_refs = refs[(nx + nw) * kq:(nx + nw) * kq + ne]
        o_refs = refs[(nx + nw) * kq + ne:(nx + nw) * kq + ne + no]
        c_refs = refs[(nx + nw) * kq + ne + no:(nx + nw) * kq + ne + no + nc]
        acc_refs = refs[(nx + nw) * kq + ne + no + nc:]
        k = pl.program_id(2)
        xv = []
        for xi in range(nx):
            pieces = []
            for p in range(kq):
                v = x_refs[xi * kq + p][...]
                if x_pro is not None:
                    v = x_pro(v.astype(F32))
                v = v.astype(BF16)
                if ragged and (nk - 1) * kq + p >= nchunk:
                    v = jnp.where(k * kq + p < nchunk, v, jnp.zeros_like(v))
                pieces.append(v)
            xv.append(pieces)
        wv = [[w_refs[wi * kq + p][...].astype(BF16) for p in range(kq)] for wi in range(nw)]
        for c_ref, wi in zip(c_refs, cast_ws):
            c_ref[...] = wv[wi][0]

        def accumulate(init):
            outs = []
            for ai, acc in enumerate(accs):
                tot = None if init is None else init[ai]
                for (xi, wi) in acc:
                    for p in range(kq):
                        for q in range(ksplit):
                            ks = slice(q * (tk // ksplit), (q + 1) * (tk // ksplit))
                            d = _dot(xv[xi][p][:, ks], wv[wi][p][ks, :])
                            tot = d if tot is None else tot + d
                outs.append(tot)
            return outs

        def finalize(vals):
            ev = []
            for r in e_refs:
                v = r[...]
                ev.append(v.reshape(v.shape[-2:]))
            outs = epi(vals, ev)
            for o_ref, o in zip(o_refs, outs):
                o_ref[...] = o.astype(o_ref.dtype).reshape(o_ref.shape)

        if nk == 1:
            finalize(accumulate(None))
        else:
            @pl.when(k == 0)
            def _():
                for a_ref, p in zip(acc_refs, accumulate(None)):
                    a_ref[...] = p

            if nk > 2:
                @pl.when(jnp.logical_and(k > 0, k < nk - 1))
                def _():
                    for a_ref, p in zip(acc_refs, accumulate([a_ref[...] for a_ref in acc_refs])):
                        a_ref[...] = p

            @pl.when(k == nk - 1)
            def _():
                finalize(accumulate([a_ref[...] for a_ref in acc_refs]))

    scratch = [pltpu.VMEM((tm, tn), F32) for _ in range(na)] if nk > 1 else []
    args = []
    for x in xs:
        args += [x] * kq
    for w in ws:
        args += [w["arr"]] * kq
    args += [e[0] for e in extras]
    outs = pl.pallas_call(
        kern, out_shape=out_shapes, grid=grid, in_specs=in_specs, out_specs=out_specs,
        scratch_shapes=scratch, name=name,
        compiler_params=_params(("parallel", "parallel", "arbitrary")),
    )(*args)
    return outs


def _ex_row(arr, tm, tn, coff=0):
    return (arr, (tm, tn), lambda i, j, k: (i, coff + j))


def _ex_col(arr, tn, coff=0):
    return (arr, (1, tn), lambda i, j, k: (0, coff + j))


class _Mod:
    def __init__(self, ada, d, per_row, row0, t_len):
        self.d, self.per_row, self.row0, self.t_len = d, per_row, row0, t_len
        rows = ada.shape[0]
        self.a2 = ada
        self.a3 = ada.reshape(rows, 1, ada.shape[1])
        self.a3r = ada.reshape(1, rows, ada.shape[1])

    def norm_operand(self, sub, kind, tt):
        slab = sub * 3 + kind
        if self.per_row:
            return self.a3r, (1, tt, self.d), (lambda b, t: (0, t, slab))
        row0 = self.row0
        return self.a3, (1, 1, self.d), (lambda b, t: (row0 + b, 0, slab))

    def mm_extra(self, sub, kind, tm, tn):
        cb = (sub * 3 + kind) * (self.d // tn)
        if self.per_row:
            return (self.a2, (tm, tn), lambda i, j, k: (i, cb + j))
        row0, t_len = self.row0, self.t_len
        return (self.a3, (1, 1, tn), lambda i, j, k: (row0 + (i * tm) // t_len, 0, cb + j))


def _adanorm(x3, g, mod, sub, out_dtype, tt, name):
    b, t, d = x3.shape
    use_mod = mod is not None

    def kern(*refs):
        if use_mod:
            x_ref, g_ref, sh_ref, sc_ref, o_ref = refs
        else:
            x_ref, g_ref, o_ref = refs
        xf = x_ref[0]
        y = xf * lax.rsqrt(jnp.mean(xf * xf, axis=-1, keepdims=True) + RMS_EPS) * g_ref[...]
        if use_mod:
            y = y * (1.0 + sc_ref[0]) + sh_ref[0]
        o_ref[0] = y.astype(o_ref.dtype)

    in_specs = [pl.BlockSpec((1, tt, d), lambda bi, ti: (bi, ti, 0)),
                pl.BlockSpec((1, d), lambda bi, ti: (0, 0))]
    args = [x3, g.reshape(1, d)]
    if use_mod:
        for kind in (0, 1):
            arr, bshape, imap = mod.norm_operand(sub, kind, tt)
            in_specs.append(pl.BlockSpec(bshape, imap))
            args.append(arr)
    return pl.pallas_call(
        kern, out_shape=jax.ShapeDtypeStruct((b, t, d), out_dtype), grid=(b, t // tt),
        in_specs=in_specs, out_specs=pl.BlockSpec((1, tt, d), lambda bi, ti: (bi, ti, 0)),
        name=name, compiler_params=_params(("parallel", "parallel")),
    )(*args)


def _rope_tables(pos, half):
    inv_freq = ROPE_THETA ** (-jnp.arange(half, dtype=F32) / half)
    ang = pos.astype(F32)[:, None] * inv_freq[None, :]
    cos, sin = jnp.cos(ang), jnp.sin(ang)
    return jnp.concatenate([cos, cos, -sin, sin], axis=-1)


def _rope_cols(w, half):
    return jnp.concatenate([w, w[..., half:], w[..., :half]], axis=-1)


def _kvprep(a, kr_raw, tab, q_norm_g, kv_norm_g, ql, kvl, rd, tm, name):
    m = a.shape[0]

    def kern(a_ref, kr_ref, tab_ref, qg_ref, kg_ref, qn_ref, ckv_ref, kro_ref, kcat_ref):
        av = a_ref[...]
        cq, ck = av[:, :ql], av[:, ql:]
        qn = cq * lax.rsqrt(jnp.mean(cq * cq, axis=-1, keepdims=True) + RMS_EPS) * qg_ref[...]
        qn_ref[...] = qn.astype(BF16)
        ckv = ck * lax.rsqrt(jnp.mean(ck * ck, axis=-1, keepdims=True) + RMS_EPS) * kg_ref[...]
        ckv_ref[...] = ckv
        r = kr_ref[...] * tab_ref[...]
        rope = r + pltpu.roll(r, rd, 1)
        kro_ref[...] = rope[:, :rd]
        lane = lax.broadcasted_iota(jnp.int32, rope.shape, 1)
        kcat_ref[...] = jnp.concatenate([ckv, jnp.where(lane < rd, rope, 0.0)], axis=1).astype(BF16)

    return pl.pallas_call(
        kern,
        out_shape=[jax.ShapeDtypeStruct((m, ql), BF16), jax.ShapeDtypeStruct((m, kvl), F32),
                   jax.ShapeDtypeStruct((m, rd), F32), jax.ShapeDtypeStruct((m, kvl + 2 * rd), BF16)],
        grid=(m // tm,),
        in_specs=[pl.BlockSpec((tm, ql + kvl), lambda i: (i, 0)),
                  pl.BlockSpec((tm, 2 * rd), lambda i: (i, 0)),
                  pl.BlockSpec((tm, 2 * rd), lambda i: (i, 0)),
                  pl.BlockSpec((1, ql), lambda i: (0, 0)),
                  pl.BlockSpec((1, kvl), lambda i: (0, 0))],
        out_specs=[pl.BlockSpec((tm, ql), lambda i: (i, 0)), pl.BlockSpec((tm, kvl), lambda i: (i, 0)),
                   pl.BlockSpec((tm, rd), lambda i: (i, 0)), pl.BlockSpec((tm, kvl + 2 * rd), lambda i: (i, 0))],
        name=name, compiler_params=_params(("parallel",)),
    )(a, kr_raw, tab, q_norm_g.reshape(1, ql), kv_norm_g.reshape(1, kvl))


def _qproj(qn, w_q, w_uk_t, tab, tm, name):
    m, ql = qn.shape
    nh, nd, kvl = w_uk_t.shape
    rd2 = w_q.shape[2] - nd

    def kern(qn_ref, wq_ref, wuk_ref, tab_ref, o_ref):
        qq = _dot(qn_ref[...], wq_ref[...].astype(BF16))
        q_lat = _dot(qq[:, :nd].astype(BF16), wuk_ref[...].astype(BF16))
        r = qq[:, nd:] * tab_ref[...]
        q_rope = r + pltpu.roll(r, rd2 // 2, 1)
        o_ref[...] = jnp.concatenate([q_lat, q_rope], axis=1).astype(BF16)

    return pl.pallas_call(
        kern, out_shape=jax.ShapeDtypeStruct((nh, m, kvl + rd2), BF16), grid=(nh, m // tm),
        in_specs=[pl.BlockSpec((tm, ql), lambda h, i: (i, 0)),
                  pl.BlockSpec((None, ql, nd + rd2), lambda h, i: (h, 0, 0)),
                  pl.BlockSpec((None, nd, kvl), lambda h, i: (h, 0, 0)),
                  pl.BlockSpec((tm, rd2), lambda h, i: (i, 0))],
        out_specs=pl.BlockSpec((None, tm, kvl + rd2), lambda h, i: (h, i, 0)),
        name=name, compiler_params=_params(("parallel", "parallel")),
    )(qn, w_q, w_uk_t, tab)


def _prompt_attention(qcat, kcat, w_uv, n_seq, t_len, scale, tq, hb, name):
    nh, m, e = qcat.shape
    _, kvl, vd = w_uv.shape
    nq = t_len // tq
    heads = range(hb)

    def kern(q_ref, k_ref, w_ref, o_ref, m_ref, l_ref, acc_ref, s_ref):
        qi = pl.program_id(2)
        m_ref[...] = jnp.full(m_ref.shape, -jnp.inf, F32)
        l_ref[...] = jnp.zeros(l_ref.shape, F32)
        acc_ref[...] = jnp.zeros(acc_ref.shape, F32)

        def scores(h, j):
            kb = k_ref[pl.ds(pl.multiple_of(j * tq, tq), tq), :]
            return _dot_nt(q_ref[h], kb) * scale

        def soft_pv(h, s, j):
            m_prev = m_ref[h]
            m_new = jnp.maximum(m_prev, jnp.max(s, axis=-1, keepdims=True))
            alpha = jnp.exp(m_prev - m_new)
            p = jnp.exp(s - m_new)
            l_ref[h] = l_ref[h] * alpha + jnp.sum(p, axis=-1, keepdims=True)
            m_ref[h] = m_new
            vb = k_ref[pl.ds(pl.multiple_of(j * tq, tq), tq), :kvl]
            acc_ref[h] = acc_ref[h] * alpha + _dot(p.astype(BF16), vb)

        for h in heads:
            s_ref[h] = scores(h, 0)

        def body(j, carry):
            ahead = min(2, hb)
            nxt = {h: scores(h, j + 1) for h in range(ahead)}
            for h in heads:
                s_cur = s_ref[h]
                s_ref[h] = nxt.pop(h)
                soft_pv(h, s_cur, j)
                if h + ahead < hb:
                    nxt[h + ahead] = scores(h + ahead, j + 1)
            return carry

        lax.fori_loop(0, qi, body, 0)
        row = lax.broadcasted_iota(jnp.int32, (tq, tq), 0)
        col = lax.broadcasted_iota(jnp.int32, (tq, tq), 1)
        for h in heads:
            soft_pv(h, jnp.where(col <= row, s_ref[h], -jnp.inf), qi)
        for h in heads:
            o_lat = (acc_ref[h] / l_ref[h]).astype(BF16)
            o_ref[:, h * vd:(h + 1) * vd] = _dot(o_lat, w_ref[h].astype(BF16)).astype(BF16)

    return pl.pallas_call(
        kern, out_shape=jax.ShapeDtypeStruct((m, nh * vd), BF16), grid=(n_seq, nh // hb, nq),
        in_specs=[pl.BlockSpec((hb, tq, e), lambda b, hg, qi: (hg, b * nq + qi, 0)),
                  pl.BlockSpec((t_len, e), lambda b, hg, qi: (b, 0)),
                  pl.BlockSpec((hb, kvl, vd), lambda b, hg, qi: (hg, 0, 0))],
        out_specs=pl.BlockSpec((tq, hb * vd), lambda b, hg, qi: (b * nq + qi, hg)),
        scratch_shapes=[pltpu.VMEM((hb, tq, 1), F32), pltpu.VMEM((hb, tq, 1), F32),
                        pltpu.VMEM((hb, tq, kvl), F32), pltpu.VMEM((hb, tq, tq), F32)],
        name=name, compiler_params=_params(("parallel", "parallel", "arbitrary")),
    )(qcat, kcat, w_uv)


def _decode_attention(q, cache_kv, cache_kr_t, layer, page_table, ckv_new, kr_new, scale, pp, name):
    bsz, nh, e = q.shape
    page, kvl = cache_kv.shape[2], cache_kv.shape[3]
    rd = cache_kr_t.shape[2]
    n_pages = page_table.shape[1]
    assert n_pages % pp == 0
    ns = n_pages // pp

    def kern(pt_ref, q_ref, *refs):
        kv_refs = refs[:pp]
        kr_refs = refs[pp:2 * pp]
        cn_ref, rn_ref, o_ref, m_ref, l_ref, acc_ref, kvb_ref, krb_ref = refs[2 * pp:]
        s_idx = pl.program_id(1)

        @pl.when(s_idx == 0)
        def _():
            m_ref[...] = jnp.full(m_ref.shape, -jnp.inf, F32)
            l_ref[...] = jnp.zeros(l_ref.shape, F32)
            acc_ref[...] = jnp.zeros(acc_ref.shape, F32)

        qv = q_ref[0]
        ql, qr = qv[:, :kvl], qv[:, kvl:kvl + rd]

        def online(s, pv_fn):
            m_prev = m_ref[...]
            m_new = jnp.maximum(m_prev, jnp.max(s, axis=-1, keepdims=True))
            alpha = jnp.exp(m_prev - m_new)
            p = jnp.exp(s - m_new)
            l_ref[...] = l_ref[...] * alpha + jnp.sum(p, axis=-1, keepdims=True)
            acc_ref[...] = acc_ref[...] * alpha + pv_fn(p)
            m_ref[...] = m_new

        for j in range(pp):
            kvb_ref[j * page:(j + 1) * page, :] = kv_refs[j][...].astype(BF16)
            krb_ref[:, j * page:(j + 1) * page] = kr_refs[j][...].astype(BF16)
        kv_all = kvb_ref[...]
        s = (_dot_nt(ql, kv_all) + _dot(qr, krb_ref[...])) * scale
        online(s, lambda p: _dot(p.astype(BF16), kv_all))

        @pl.when(s_idx == ns - 1)
        def _():
            cn = cn_ref[0].astype(BF16).astype(F32)
            rn = rn_ref[0].astype(BF16).astype(F32)
            s_new = (jnp.sum(ql.astype(F32) * cn, axis=-1, keepdims=True)
                     + jnp.sum(qr.astype(F32) * rn, axis=-1, keepdims=True)) * scale
            online(s_new, lambda p: p.astype(BF16).astype(F32) * cn)
            o_ref[0] = acc_ref[...] / l_ref[...]

    in_specs = [pl.BlockSpec((1, nh, e), lambda b, s, pt: (b, 0, 0))]
    for j in range(pp):
        in_specs.append(pl.BlockSpec((None, None, page, kvl),
                                     lambda b, s, pt, j=j: (layer, pt[b, s * pp + j], 0, 0)))
    for j in range(pp):
        in_specs.append(pl.BlockSpec((None, None, rd, page),
                                     lambda b, s, pt, j=j: (layer, pt[b, s * pp + j], 0, 0)))
    in_specs.append(pl.BlockSpec((1, 1, kvl), lambda b, s, pt: (b, 0, 0)))
    in_specs.append(pl.BlockSpec((1, 1, rd), lambda b, s, pt: (b, 0, 0)))
    grid_spec = pltpu.PrefetchScalarGridSpec(
        num_scalar_prefetch=1, grid=(bsz, ns), in_specs=in_specs,
        out_specs=pl.BlockSpec((1, nh, kvl), lambda b, s, pt: (b, 0, 0)),
        scratch_shapes=[pltpu.VMEM((nh, 1), F32), pltpu.VMEM((nh, 1), F32), pltpu.VMEM((nh, kvl), F32),
                        pltpu.VMEM((pp * page, kvl), BF16), pltpu.VMEM((rd, pp * page), BF16)])
    return pl.pallas_call(
        kern, out_shape=jax.ShapeDtypeStruct((bsz, nh, kvl), F32), grid_spec=grid_spec, name=name,
        compiler_params=_params(("parallel", "arbitrary")),
    )(page_table, q, *([cache_kv] * pp), *([cache_kr_t] * pp), ckv_new, kr_new)


def _oproj(o_lat, w_uv, tm, name):
    nh, m, kvl = o_lat.shape
    vd = w_uv.shape[2]

    def kern(o_ref, w_ref, y_ref):
        y_ref[...] = _dot(o_ref[...].astype(BF16), w_ref[...].astype(BF16)).astype(BF16)

    return pl.pallas_call(
        kern, out_shape=jax.ShapeDtypeStruct((m, nh * vd), BF16), grid=(nh, m // tm),
        in_specs=[pl.BlockSpec((None, tm, kvl), lambda h, i: (h, i, 0)),
                  pl.BlockSpec((None, kvl, vd), lambda h, i: (h, 0, 0))],
        out_specs=pl.BlockSpec((tm, vd), lambda h, i: (i, h)),
        name=name, compiler_params=_params(("parallel", "parallel")),
    )(o_lat, w_uv)


def _conv_prompt(bcu, conv_w, n_seq, t_len, cd, tc, name):
    m = bcu.shape[0]
    ncb = cd // tc
    cw = conv_w.shape[0]

    def kern(b_ref, c_ref, u_ref, w_ref, y_ref, st_ref):
        xg = c_ref[...] * u_ref[...]
        w = w_ref[...]
        row = lax.broadcasted_iota(jnp.int32, xg.shape, 0)
        y = xg * w[cw - 1:cw]
        for d in range(1, cw):
            sh = jnp.where(row >= d, pltpu.roll(xg, d, 0), 0.0)
            y = y + sh * w[cw - 1 - d:cw - d]
        y_ref[...] = (b_ref[...] * y).astype(BF16)
        st_ref[0] = xg[t_len - (cw - 1):, :]

    return pl.pallas_call(
        kern, out_shape=[jax.ShapeDtypeStruct((m, cd), BF16), jax.ShapeDtypeStruct((n_seq, cw - 1, cd), F32)],
        grid=(n_seq, ncb),
        in_specs=[pl.BlockSpec((t_len, tc), lambda b, j: (b, j)),
                  pl.BlockSpec((t_len, tc), lambda b, j: (b, ncb + j)),
                  pl.BlockSpec((t_len, tc), lambda b, j: (b, 2 * ncb + j)),
                  pl.BlockSpec((cw, tc), lambda b, j: (0, j))],
        out_specs=[pl.BlockSpec((t_len, tc), lambda b, j: (b, j)),
                   pl.BlockSpec((1, cw - 1, tc), lambda b, j: (b, 0, j))],
        name=name, compiler_params=_params(("parallel", "parallel")),
    )(bcu, bcu, bcu, conv_w)


def _conv_step(bcu, prev, conv_w, cd, tc, name):
    bsz = bcu.shape[0]
    ncb = cd // tc
    cw = conv_w.shape[0]

    def kern(*refs):
        b_ref, c_ref, u_ref, w_ref = refs[:4]
        p_refs = refs[4:4 + cw - 1]
        y_ref = refs[4 + cw - 1]
        st_refs = refs[5 + cw - 1:]
        xg = c_ref[...] * u_ref[...]
        w = w_ref[...]
        y = xg * w[cw - 1:cw]
        for d in range(cw - 1):
            y = y + p_refs[d][...] * w[d:d + 1]
        y_ref[...] = (b_ref[...] * y).astype(BF16)
        for d in range(cw - 2):
            st_refs[d][...] = p_refs[d + 1][...]
        st_refs[cw - 2][...] = xg

    in_specs = [pl.BlockSpec((bsz, tc), lambda j: (0, j)),
                pl.BlockSpec((bsz, tc), lambda j: (0, ncb + j)),
                pl.BlockSpec((bsz, tc), lambda j: (0, 2 * ncb + j)),
                pl.BlockSpec((cw, tc), lambda j: (0, j))]
    for d in range(cw - 1):
        in_specs.append(pl.BlockSpec((bsz, tc), lambda j, d=d: (0, d * ncb + j)))
    outs = pl.pallas_call(
        kern,
        out_shape=[jax.ShapeDtypeStruct((bsz, cd), BF16)] + [jax.ShapeDtypeStruct((bsz, cd), F32)] * (cw - 1),
        grid=(ncb,), in_specs=in_specs,
        out_specs=[pl.BlockSpec((bsz, tc), lambda j: (0, j))] * cw,
        name=name, compiler_params=_params(("parallel",)),
    )(bcu, bcu, bcu, conv_w, *([prev] * (cw - 1)))
    return outs[0], jnp.stack(outs[1:], axis=1)


def _token_mix(h3, prev3, mu, tc, name):
    b, t, d = h3.shape
    nmix = mu.shape[0]
    shift_here = prev3 is None

    def kern(*refs):
        if shift_here:
            h_ref, mu_ref = refs[:2]
            o_refs = refs[2:]
        else:
            h_ref, p_ref, mu_ref = refs[:3]
            o_refs = refs[3:]
        hh = h_ref[0]
        if shift_here:
            row = lax.broadcasted_iota(jnp.int32, hh.shape, 0)
            prev = jnp.where(row >= 1, pltpu.roll(hh, 1, 0), 0.0)
        else:
            prev = p_ref[0]
        xx = prev - hh
        mv = mu_ref[...]
        for i in range(nmix):
            o_refs[i][0] = (hh + xx * mv[i:i + 1]).astype(BF16)

    blk = pl.BlockSpec((1, t, tc), lambda bi, j: (bi, 0, j))
    in_specs = [blk] + ([] if shift_here else [blk]) + [pl.BlockSpec((nmix, tc), lambda bi, j: (0, j))]
    args = [h3] + ([] if shift_here else [prev3]) + [mu]
    return pl.pallas_call(
        kern, out_shape=[jax.ShapeDtypeStruct((b, t, d), BF16)] * nmix, grid=(b, d // tc),
        in_specs=in_specs, out_specs=[blk] * nmix,
        name=name, compiler_params=_params(("parallel", "parallel")),
    )(*args)


def _rowsum_mxu(xs, ones):
    his = [x.astype(BF16) for x in xs]
    los = [(x - hi.astype(F32)).astype(BF16) for x, hi in zip(xs, his)]
    return [_dot(hi, ones) + _dot(lo, ones) for hi, lo in zip(his, los)]


def _wkv_prompt(r, k, v, a, lw, g, k_k, k_a, r_k, ln_w, ln_b, n_seq, t_len, hs, name):
    m, d = r.shape
    c = WKV_CHUNK
    hl = min(WKV_LANES, d)
    hpb = hl // hs
    nh = d // hs
    nc = t_len // c

    def kern(r_ref, k_ref, v_ref, a_ref, lw_ref, g_ref, kk_ref, ka_ref, rk_ref, lnw_ref, lnb_ref,
             o_ref, s_ref):
        ci = pl.program_id(2)

        @pl.when(ci == 0)
        def _():
            s_ref[...] = jnp.zeros(s_ref.shape, F32)

        ri = lax.broadcasted_iota(jnp.int32, (c, c), 0)
        cj = lax.broadcasted_iota(jnp.int32, (c, c), 1)
        strict = cj < ri
        ri2 = lax.broadcasted_iota(jnp.int32, (c, 2 * c), 0)
        cj2 = lax.broadcasted_iota(jnp.int32, (c, 2 * c), 1)
        incl2 = jnp.where(cj2 >= c, cj2 - c, cj2) <= ri2
        tri = jnp.where(cj <= ri, 1.0, 0.0).astype(BF16)
        eye = jnp.where(cj == ri, 1.0, 0.0).astype(F32)
        heads = range(hpb)
        sls = [slice(hh * hs, (hh + 1) * hs) for hh in heads]

        lw_all = lw_ref[...]
        lw_hi = lw_all.astype(BF16)
        lw_lo = (lw_all - lw_hi.astype(F32)).astype(BF16)
        cum_all = _dot(tri, lw_hi) + _dot(tri, lw_lo)
        ones = jnp.ones((hs, hs), BF16)
        kk0 = [k_ref[:, sl] * kk_ref[:, sl] for sl in sls]
        ssq = _rowsum_mxu([x * x for x in kk0], ones)
        at, rt, bk_t, bk_h, vb, kmods, gtot = [], [], [], [], [], [], []
        for hh in heads:
            sl = sls[hh]
            rr, k0, aa, lwh = r_ref[:, sl], k_ref[:, sl], a_ref[:, sl], lw_all[:, sl]
            kk = kk0[hh] * lax.rsqrt(jnp.maximum(ssq[hh], 1e-24))
            kmod = k0 * (1.0 + (aa - 1.0) * ka_ref[:, sl])
            avec, bvec = -kk, kk * aa
            cum = cum_all[:, sl]
            tot = cum[c - 1:c, :]
            at.append((avec * jnp.exp(cum - lwh)).astype(BF16))
            rt.append((rr * jnp.exp(cum)).astype(BF16))
            e_neg = jnp.exp(-cum)
            e_tot = jnp.exp(tot - cum)
            bk_t.append(jnp.concatenate([bvec * e_neg, kmod * e_neg], axis=0).astype(BF16))
            bk_h.append(jnp.concatenate([bvec * e_tot, kmod * e_tot], axis=0).astype(BF16))
            vb.append(v_ref[:, sl].astype(BF16))
            kmods.append(kmod)
            gtot.append(jnp.exp(tot))
        a_ab = [jnp.where(strict, _dot_nt(at[hh], bk_t[hh][:c]), 0.0) for hh in heads]
        a_ak = [jnp.where(strict, _dot_nt(at[hh], bk_t[hh][c:]), 0.0).astype(BF16) for hh in heads]
        a_r = [jnp.where(incl2, _dot_nt(rt[hh], bk_t[hh]), 0.0).astype(BF16) for hh in heads]
        s0 = [s_ref[0, hh] for hh in heads]
        ws = [_dot_nt(jnp.concatenate([at[hh], rt[hh]], axis=0), s0[hh].astype(BF16)) for hh in heads]
        akv = [_dot(a_ak[hh], vb[hh]) for hh in heads]
        tinv = [eye + a_ab[hh] for hh in heads]
        lp = [a_ab[hh].astype(BF16) for hh in heads]
        n_done = 2
        while n_done < c:
            lp = [_dot(lp[hh], lp[hh]).astype(BF16) for hh in heads]
            tinv = [tinv[hh] + _dot(tinv[hh].astype(BF16), lp[hh]) for hh in heads]
            n_done *= 2
        p = [_dot(tinv[hh].astype(BF16), (ws[hh][:c] + akv[hh]).astype(BF16)) for hh in heads]
        z = [jnp.concatenate([p[hh].astype(BF16), vb[hh]], axis=0) for hh in heads]
        y = [ws[hh][c:] + _dot(a_r[hh], z[hh]) for hh in heads]
        s1 = [s0[hh] * gtot[hh] + _dot_tn(z[hh], bk_h[hh]) for hh in heads]
        rs = _rowsum_mxu([jnp.concatenate([y[hh], y[hh] * y[hh], r_ref[:, sls[hh]] * kmods[hh] * rk_ref[:, sls[hh]]],
                                          axis=0) for hh in heads], ones)
        for hh in heads:
            sl = sls[hh]
            s_ref[0, hh] = s1[hh]
            mean = rs[hh][:c] * (1.0 / hs)
            var = jnp.maximum(rs[hh][c:2 * c] * (1.0 / hs) - mean * mean, 0.0)
            yn = (y[hh] - mean) * lax.rsqrt(var + GN_EPS) * lnw_ref[:, sl] + lnb_ref[:, sl]
            o_ref[:, sl] = ((yn + rs[hh][2 * c:] * v_ref[:, sl]) * g_ref[:, sl]).astype(BF16)

    blk = pl.BlockSpec((c, hl), lambda b, hg, ci: (b * nc + ci, hg))
    wblk = pl.BlockSpec((1, hl), lambda b, hg, ci: (0, hg))
    return pl.pallas_call(
        kern, out_shape=[jax.ShapeDtypeStruct((m, d), BF16), jax.ShapeDtypeStruct((n_seq, nh, hs, hs), F32)],
        grid=(n_seq, d // hl, nc),
        in_specs=[blk] * 6 + [wblk] * 5,
        out_specs=[blk, pl.BlockSpec((1, hpb, hs, hs), lambda b, hg, ci: (b, hg, 0, 0))],
        name=name, compiler_params=_params(("parallel", "parallel", "arbitrary")),
    )(r, k, v, a, lw, g, k_k, k_a, r_k, ln_w, ln_b)


def _wkv_step(r, k, a, lw, v, g, s0, k_k, k_a, r_k, ln_w, ln_b, hb, name):
    nh, hs, bsz = r.shape

    def kern(r_ref, k_ref, a_ref, lw_ref, v_ref, g_ref, s_ref, kk_ref, ka_ref, rk_ref, lnw_ref, lnb_ref,
             o_ref, so_ref):
        rr, k0, aa = r_ref[...], k_ref[...], a_ref[...]
        vv = v_ref[...]
        dec = jnp.exp(lw_ref[...])
        kk = k0 * kk_ref[...]
        kk = kk * lax.rsqrt(jnp.maximum(jnp.sum(kk * kk, axis=1, keepdims=True), 1e-24))
        kmod = k0 * (1.0 + (aa - 1.0) * ka_ref[...])
        avec, bvec = -kk, kk * aa
        s = s_ref[...]
        sa = jnp.sum(s * avec[:, None], axis=2, keepdims=True)
        s1 = s * dec[:, None] + sa * bvec[:, None] + vv * kmod[:, None]
        so_ref[...] = s1
        y = jnp.sum(s1 * rr[:, None], axis=2, keepdims=True)
        mean = jnp.mean(y, axis=1, keepdims=True)
        var = jnp.mean(jnp.square(y - mean), axis=1, keepdims=True)
        yn = (y - mean) * lax.rsqrt(var + GN_EPS) * lnw_ref[...] + lnb_ref[...]
        bonus = jnp.sum(rr * kmod * rk_ref[...], axis=1, keepdims=True)[:, None] * vv
        o_ref[...] = (yn + bonus) * g_ref[...]

    kblk = pl.BlockSpec((hb, hs, bsz), lambda i: (i, 0, 0))
    vblk = pl.BlockSpec((hb, hs, 1, bsz), lambda i: (i, 0, 0, 0))
    sblk = pl.BlockSpec((hb, hs, hs, bsz), lambda i: (i, 0, 0, 0))
    wkblk = pl.BlockSpec((hb, hs, 1), lambda i: (i, 0, 0))
    wvblk = pl.BlockSpec((hb, hs, 1, 1), lambda i: (i, 0, 0, 0))
    return pl.pallas_call(
        kern, out_shape=[jax.ShapeDtypeStruct((nh, hs, 1, bsz), F32), jax.ShapeDtypeStruct((nh, hs, hs, bsz), F32)],
        grid=(nh // hb,),
        in_specs=[kblk] * 4 + [vblk] * 2 + [sblk] + [wkblk] * 3 + [wvblk] * 2,
        out_specs=[vblk, sblk],
        name=name, compiler_params=_params(("parallel",)),
    )(r, k, a, lw, v, g, s0, k_k, k_a, r_k, ln_w, ln_b)


def _pick(n, candidates):
    for c in candidates:
        if n % c == 0:
            return c
    return n


def kernel(x_prompt, x_sample, cache_kv_latent, cache_k_rope, state_conv, state_rwkv_shift, state_rwkv_wkv, page_table, c_prompt, c_sample, w_ada, b_ada, norm_g, ffn_w1, ffn_w3, ffn_w2, mla_w_in, mla_q_norm, mla_kv_norm, mla_w_uq, mla_w_ukv, conv_w, mix_w_out, rwkv_mu, rwkv_w0, rwkv_w1, rwkv_w2, rwkv_a0, rwkv_a1, rwkv_a2, rwkv_g1, rwkv_g2, rwkv_k_k, rwkv_k_a, rwkv_r_k, rwkv_w_r, rwkv_w_k, rwkv_w_v, rwkv_w_o, rwkv_ln_w, rwkv_ln_b, final_norm_g):
    nb, seq, d = x_prompt.shape
    nsb, dec_seq, _ = x_sample.shape
    assert dec_seq == 1
    depth = w_ada.shape[0]
    dff = ffn_w1.shape[-1]
    ql = mla_q_norm.shape[1]
    kvl = mla_kv_norm.shape[1]
    rd = cache_k_rope.shape[-1]
    nh_mla = mla_w_ukv.shape[2]
    nd = mla_w_uq.shape[2] // nh_mla - rd
    vd = mla_w_ukv.shape[3] - nd
    cd = conv_w.shape[-1]
    cw = conv_w.shape[1]
    hs = state_rwkv_wkv.shape[-1]
    nh_rwkv = d // hs
    past_len = page_table.shape[1] * cache_kv_latent.shape[2]
    mla_scale = float((nd + rd) ** -0.5)
    mp = nb * seq

    n_c = nsb + nb
    n_c_pad = -(-n_c // 16) * 16
    c_all = jnp.concatenate([c_sample, c_prompt, jnp.zeros((n_c_pad - n_c, d), F32)], axis=0)
    t_ada = _pick(9 * d, (512, 256, 128))
    adas = []
    for layer in range(depth):
        (ada,) = _mm([c_all], [_w(w_ada, (layer,))], [[(0, 0)]],
                     lambda acc, ev: (acc[0] + ev[0],), [F32],
                     n=9 * d, tm=n_c_pad, tn=t_ada, x_pro=_silu,
                     extras=[_ex_col(b_ada[layer].reshape(1, 9 * d), t_ada)], name=f"ada_l{layer}")
        adas.append(ada)

    ffn_bf16 = {}
    w_bf16 = {}

    def run_group(x2, is_sample):
        m = x2.shape[0]
        if is_sample:
            n_seq, t_len = m, 1
            xshape = (1, m, d)
            tm = m
            tt = m
            pos = jnp.full((m,), past_len, jnp.int32)
        else:
            n_seq, t_len = nb, seq
            xshape = (nb, seq, d)
            tm = _pick(seq, (1024, 512, 256, 128))
            tt = _pick(seq, (256, 128))
            pos = jnp.tile(jnp.arange(seq, dtype=jnp.int32), nb)
        tn = _pick(d, (512, 256, 128))
        tab = _rope_tables(pos, rd // 2)
        kv_rows, kr_rows, conv_states, shift_states, wkv_states = [], [], [], [], []

        def mm_shared(key, x, make_w, epi, out_dtype, **kw):
            if is_sample:
                o, wb = _mm([x], [make_w()], [[(0, 0)]], epi, [out_dtype], cast_ws=(0,), **kw)
                w_bf16[key] = wb
            else:
                (o,) = _mm([x], [_w(w_bf16[key])], [[(0, 0)]], epi, [out_dtype], **kw)
            return o

        def ffn(x2, mod, layer, idx, sub, tag):
            h = _adanorm(x2.reshape(xshape), norm_g[layer, sub], mod, sub, BF16, tt, f"norm_{tag}").reshape(m, d)
            tnf = _pick(dff, (256, 128))
            up_epi = lambda acc, ev: (_silu(acc[0]) * acc[1],)
            down_epi = lambda acc, ev: (ev[0] + FFN_RES * ev[1] * acc[0],)
            if is_sample:
                gact, w1b, w3b = _mm([h], [_w(ffn_w1, (layer, idx)), _w(ffn_w3, (layer, idx))],
                                     [[(0, 0)], [(0, 1)]], up_epi, [BF16],
                                     n=dff, tm=tm, tn=tnf, cast_ws=(0, 1), name=f"ffn_up_{tag}")
                tnd = _pick(d, (256, 128))
                xo, w2b = _mm([gact], [_w(ffn_w2, (layer, idx))], [[(0, 0)]], down_epi, [F32],
                              n=d, tm=tm, tn=tnd, cast_ws=(0,),
                              extras=[_ex_row(x2, tm, tnd), mod.mm_extra(sub, 2, tm, tnd)], name=f"ffn_down_{tag}")
                ffn_bf16[(layer, idx)] = (w1b, w3b, w2b)
            else:
                w1b, w3b, w2b = ffn_bf16[(layer, idx)]
                (gact,) = _mm([h], [_w(w1b), _w(w3b)], [[(0, 0)], [(0, 1)]], up_epi, [BF16],
                              n=dff, tm=tm, tn=tnf, name=f"ffn_up_{tag}")
                tnd = _pick(d, (256, 128))
                (xo,) = _mm([gact], [_w(w2b)], [[(0, 0)]], down_epi, [F32],
                            n=d, tm=tm, tn=tnd, x_resident=True, ksplit=2 if dff % 256 == 0 else 1,
                            extras=[_ex_row(x2, tm, tnd), mod.mm_extra(sub, 2, tm, tnd)], name=f"ffn_down_{tag}")
            return xo

        for layer in range(depth):
            tag = f"{'s' if is_sample else 'p'}{layer}"
            mod = _Mod(adas[layer], d, is_sample, nsb, t_len)
            x2 = ffn(x2, mod, layer, 0, 0, tag + "a")
            if layer % 2 == 0:
                i = layer // 2
                h = _adanorm(x2.reshape(xshape), norm_g[layer, 1], mod, 1, BF16, tt, f"norm_mix_{tag}").reshape(m, d)
                n_a = ql + kvl
                (a_proj,) = _mm([h], [_w(mla_w_in, (i,))], [[(0, 0)]], lambda acc, ev: (acc[0],), [F32],
                                n=n_a, tm=tm, tn=_pick(n_a, (512, 256, 128)), name=f"mla_in_a_{tag}")
                w_kr = _rope_cols(mla_w_in[i, :, n_a:n_a + rd], rd // 2)
                (kr_raw,) = _mm([h], [_w(w_kr)], [[(0, 0)]], lambda acc, ev: (acc[0],), [F32],
                                n=2 * rd, tm=tm, tn=2 * rd, name=f"mla_in_kr_{tag}")
                bcu = mm_shared(("bcu", i), h, lambda: _w(mla_w_in[i, :, n_a + rd:]), lambda acc, ev: (acc[0],), F32,
                                n=3 * cd, tm=tm, tn=_pick(cd, (512, 256, 128)), name=f"mla_in_bcu_{tag}")
                qn, ckv, kr, kcat = _kvprep(a_proj, kr_raw, tab, mla_q_norm[i], mla_kv_norm[i], ql, kvl, rd,
                                            _pick(m, (256, 128)), f"kvprep_{tag}")
                w_uq = mla_w_uq[i].reshape(ql, nh_mla, nd + rd)
                w_uq_h = jnp.transpose(w_uq, (1, 0, 2))
                w_q = jnp.concatenate([w_uq_h[:, :, :nd], _rope_cols(w_uq_h[:, :, nd:], rd // 2)], axis=-1)
                w_uk_t = jnp.transpose(mla_w_ukv[i][:, :, :nd], (1, 2, 0))
                w_uv = jnp.transpose(mla_w_ukv[i][:, :, nd:], (1, 0, 2))
                qcat = _qproj(qn, w_q, w_uk_t, tab, _pick(m, (1024, 512, 256, 128)), f"qproj_{tag}")
                if is_sample:
                    o_lat = _decode_attention(jnp.transpose(qcat, (1, 0, 2)), cache_kv_latent,
                                              jnp.swapaxes(cache_k_rope, 2, 3), i,
                                              page_table, ckv.reshape(m, 1, kvl), kr.reshape(m, 1, rd),
                                              mla_scale, _pick(page_table.shape[1], (16, 8, 4, 2, 1)), f"attn_{tag}")
                    o_att = _oproj(jnp.transpose(o_lat, (1, 0, 2)), w_uv, _pick(m, (512, 256, 128)), f"oproj_{tag}")
                    prev = state_conv[i].reshape(m, (cw - 1) * cd)
                    conv_y, conv_new = _conv_step(bcu, prev, conv_w[i], cd, _pick(cd, (512, 256, 128)), f"conv_{tag}")
                else:
                    o_att = _prompt_attention(qcat, kcat, w_uv, n_seq, t_len, mla_scale,
                                              _pick(t_len, (ATTN_TQ, 128)), _pick(nh_mla, (ATTN_HEADS, 2, 1)),
                                              f"attn_{tag}")
                    conv_y, conv_new = _conv_prompt(bcu, conv_w[i], n_seq, t_len, cd, _pick(cd, (256, 128)),
                                                    f"conv_{tag}")
                assert nh_mla * vd == cd
                (x2,) = _mm([o_att, conv_y], [_w(mix_w_out, (i,)), _w(mix_w_out, (i,), roff=1)], [[(0, 0), (1, 1)]],
                            lambda acc, ev: (ev[0] + ev[1] * acc[0],), [F32],
                            n=d, tm=tm, tn=tn, extras=[_ex_row(x2, tm, tn), mod.mm_extra(1, 2, tm, tn)],
                            name=f"mix_out_{tag}")
                kv_rows.append(ckv.reshape(n_seq, t_len, kvl))
                kr_rows.append(kr.reshape(n_seq, t_len, rd))
                conv_states.append(conv_new)
            else:
                j = layer // 2
                h3 = _adanorm(x2.reshape(xshape), norm_g[layer, 1], mod, 1, F32, tt, f"norm_mix_{tag}")
                tcm = _pick(d, (256, 128))
                if is_sample:
                    mixes = _token_mix(h3, state_rwkv_shift[j].reshape(xshape), rwkv_mu[j], tcm, f"tmix_{tag}")
                    shift_states.append(h3.reshape(m, d))
                else:
                    mixes = _token_mix(h3, None, rwkv_mu[j], tcm, f"tmix_{tag}")
                    shift_states.append(h3[:, -1])
                xr, xw, xk, xv, xa, xg = (t.reshape(m, d) for t in mixes)

                def proj(x, w, tag2):
                    return mm_shared((tag2, j), x, lambda: _w(w, (j,)), lambda acc, ev: (acc[0],), F32,
                                     n=d, tm=tm, tn=tn, name=f"rwkv_{tag2}_{tag}")

                r, k, v = proj(xr, rwkv_w_r, "r"), proj(xk, rwkv_w_k, "k"), proj(xv, rwkv_w_v, "v")
                dl = rwkv_w1.shape[-1]
                (t1,) = _mm([xw], [_w(rwkv_w1, (j,))], [[(0, 0)]], lambda acc, ev: (jnp.tanh(acc[0]),), [BF16],
                            n=dl, tm=tm, tn=dl, name=f"rwkv_w1_{tag}")

                def decay_epi(acc, ev):
                    z = -(ev[0] + acc[0])
                    softplus = jnp.maximum(z, 0.0) + jnp.log(1.0 + jnp.exp(-jnp.abs(z)))
                    return (-jnp.exp(-softplus - 0.5),)

                (lw,) = _mm([t1], [_w(rwkv_w2, (j,))], [[(0, 0)]], decay_epi, [F32],
                            n=d, tm=tm, tn=tn, extras=[_ex_col(rwkv_w0[j].reshape(1, d), tn)], name=f"rwkv_w2_{tag}")
                al = rwkv_a1.shape[-1]
                (t2,) = _mm([xa], [_w(rwkv_a1, (j,))], [[(0, 0)]], lambda acc, ev: (acc[0],), [BF16],
                            n=al, tm=tm, tn=al, name=f"rwkv_a1_{tag}")
                (a,) = _mm([t2], [_w(rwkv_a2, (j,))], [[(0, 0)]], lambda acc, ev: (_sigmoid(ev[0] + acc[0]),), [F32],
                           n=d, tm=tm, tn=tn, extras=[_ex_col(rwkv_a0[j].reshape(1, d), tn)], name=f"rwkv_a2_{tag}")
                gl = rwkv_g1.shape[-1]
                glp = -(-gl // 128) * 128
                g1p = jnp.pad(rwkv_g1[j], ((0, 0), (0, glp - gl)))
                g2p = jnp.pad(rwkv_g2[j], ((0, glp - gl), (0, 0)))
                (t3,) = _mm([xg], [_w(g1p)], [[(0, 0)]], lambda acc, ev: (_sigmoid(acc[0]),), [BF16],
                            n=glp, tm=tm, tn=glp, name=f"rwkv_g1_{tag}")
                (g,) = _mm([t3], [_w(g2p)], [[(0, 0)]], lambda acc, ev: (acc[0],), [F32],
                           n=d, tm=tm, tn=tn, name=f"rwkv_g2_{tag}")
                if is_sample:
                    tk3 = lambda t: t.T.reshape(nh_rwkv, hs, m)
                    tv4 = lambda t: t.T.reshape(nh_rwkv, hs, 1, m)
                    wk3 = lambda t: t.reshape(nh_rwkv, hs, 1)
                    wv4 = lambda t: t.reshape(nh_rwkv, hs, 1, 1)
                    o4, s_t = _wkv_step(tk3(r), tk3(k), tk3(a), tk3(lw), tv4(v), tv4(g),
                                        jnp.transpose(state_rwkv_wkv[j], (1, 2, 3, 0)),
                                        wk3(rwkv_k_k[j]), wk3(rwkv_k_a[j]), wk3(rwkv_r_k[j]),
                                        wv4(rwkv_ln_w[j]), wv4(rwkv_ln_b[j]),
                                        _pick(nh_rwkv, (2, 1)), f"wkv_{tag}")
                    o_pre = o4.reshape(d, m).T
                    s_new = jnp.transpose(s_t, (3, 0, 1, 2))
                else:
                    w2d = lambda t: t.reshape(1, d)
                    o_pre, s_new = _wkv_prompt(r, k, v, a, lw, g, w2d(rwkv_k_k[j]), w2d(rwkv_k_a[j]),
                                               w2d(rwkv_r_k[j]), w2d(rwkv_ln_w[j]), w2d(rwkv_ln_b[j]),
                                               n_seq, t_len, hs, f"wkv_{tag}")
                x2 = mm_shared(("o", j), o_pre, lambda: _w(rwkv_w_o, (j,)),
                               lambda acc, ev: (ev[0] + ev[1] * acc[0],), F32,
                               n=d, tm=tm, tn=tn, extras=[_ex_row(x2, tm, tn), mod.mm_extra(1, 2, tm, tn)],
                               name=f"rwkv_o_{tag}")
                wkv_states.append(s_new)
            x2 = ffn(x2, mod, layer, 1, 2, tag + "b")
        y = _adanorm(x2.reshape(xshape), final_norm_g, None, 0, F32, tt, f"final_norm_{'s' if is_sample else 'p'}")
        return (y.reshape(n_seq, t_len, d), jnp.stack(kv_rows), jnp.stack(kr_rows), jnp.stack(conv_states),
                jnp.stack(shift_states), jnp.stack(wkv_states))

    y_s, s_kv, s_kr, s_conv, s_shift, s_wkv = run_group(x_sample.reshape(nsb, d), True)
    y_p, p_kv, p_kr, p_conv, p_shift, p_wkv = run_group(x_prompt.reshape(mp, d), False)
    return (y_p, y_s, p_kv, p_kr, p_conv, p_shift, p_wkv, s_kv, s_kr, s_conv, s_shift, s_wkv)
```

```python
import functools

import jax
import jax.numpy as jnp
from jax import lax
from jax.experimental import pallas as pl
from jax.experimental.pallas import tpu as pltpu

F32 = jnp.float32
BF16 = jnp.bfloat16

RMS_EPS = 1e-6
GN_EPS = 64e-5
ROPE_THETA = 10000.0
FFN_RES = 0.5
WKV_CHUNK = 64
WKV_LANES = 1024
ATTN_TQ = 512
ATTN_HEADS = 4
V7X_VMEM_LIMIT_BYTES = 56 * 1024 * 1024
HI = lax.Precision.HIGHEST


def _params(sem):
    return pltpu.CompilerParams(dimension_semantics=sem, vmem_limit_bytes=V7X_VMEM_LIMIT_BYTES)


def _dot(a, b, precision=None):
    return jnp.dot(a, b, preferred_element_type=F32, precision=precision)


def _dot_nt(a, b, precision=None):
    return lax.dot_general(a, b, (((1,), (1,)), ((), ())), preferred_element_type=F32, precision=precision)


def _dot_tn(a, b, precision=None):
    return lax.dot_general(a, b, (((0,), (0,)), ((), ())), preferred_element_type=F32, precision=precision)


def _sigmoid(x):
    return 1.0 / (1.0 + jnp.exp(-x))


def _silu(x):
    return x * _sigmoid(x)


def _w(arr, lead=(), roff=0, coff=0):
    return dict(arr=arr, lead=tuple(lead), roff=roff, coff=coff)


def _mm(xs, ws, accs, epi, out_dtypes, *, n, tm, tn, tk=None, kq=1, extras=(), x_pro=None, cast_ws=(),
        x_resident=False, ksplit=1, name):
    m, kdim = xs[0].shape
    tk = kdim if tk is None else tk
    assert m % tm == 0 and kdim % tk == 0, (name, m, tm, kdim, tk)
    assert not cast_ws or (m == tm and kq == 1)
    nchunk = kdim // tk
    nk = pl.cdiv(nchunk, kq)
    ragged = nk * kq != nchunk
    grid = (m // tm, pl.cdiv(n, tn), nk)
    nx, nw, ne, na, no = len(xs), len(ws), len(extras), len(accs), len(out_dtypes)

    def chunk(k, p):
        c = k * kq + p
        return jnp.minimum(c, nchunk - 1) if ragged else c

    in_specs = []
    for _ in xs:
        for p in range(kq):
            mode = dict(pipeline_mode=pl.Buffered(1)) if x_resident else {}
            in_specs.append(pl.BlockSpec((tm, tk), lambda i, j, k, p=p: (i, chunk(k, p)), **mode))
    for w in ws:
        nlead = len(w["lead"])
        assert w["arr"].ndim == nlead + 2
        for p in range(kq):
            def w_map(i, j, k, w=w, p=p):
                return w["lead"] + (w["roff"] + chunk(k, p), w["coff"] + j)

            in_specs.append(pl.BlockSpec((None,) * nlead + (tk, tn), w_map))
    for (_, bshape, imap) in extras:
        in_specs.append(pl.BlockSpec(bshape, imap))
    out_shapes = [jax.ShapeDtypeStruct((m, n), dt) for dt in out_dtypes]
    out_specs = [pl.BlockSpec((tm, tn), lambda i, j, k: (i, j)) for _ in out_dtypes]
    for wi in cast_ws:
        out_shapes.append(jax.ShapeDtypeStruct(ws[wi]["arr"].shape[-2:], BF16))
        out_specs.append(pl.BlockSpec((tk, tn), lambda i, j, k: (k, j)))
    nc = len(cast_ws)

    def kern(*refs):
        x_refs = refs[:nx * kq]
        w_refs = refs[nx * kq:(nx + nw) * kq]
        e_refs = refs[(nx + nw) * kq:(nx + nw) * kq + ne]
        o_refs = refs[(nx + nw) * kq + ne:(nx + nw) * kq + ne + no]
        c_refs = refs[(nx + nw) * kq + ne + no:(nx + nw) * kq + ne + no + nc]
        acc_refs = refs[(nx + nw) * kq + ne + no + nc:]
        k = pl.program_id(2)
        xv = []
        for xi in range(nx):
            pieces = []
            for p in range(kq):
                v = x_refs[xi * kq + p][...]
                if x_pro is not None:
                    v = x_pro(v.astype(F32))
                v = v.astype(BF16)
                if ragged and (nk - 1) * kq + p >= nchunk:
                    v = jnp.where(k * kq + p < nchunk, v, jnp.zeros_like(v))
                pieces.append(v)
            xv.append(pieces)
        wv = [[w_refs[wi * kq + p][...].astype(BF16) for p in range(kq)] for wi in range(nw)]
        for c_ref, wi in zip(c_refs, cast_ws):
            c_ref[...] = wv[wi][0]

        def accumulate(init):
            outs = []
            for ai, acc in enumerate(accs):
                tot = None if init is None else init[ai]
                for (xi, wi) in acc:
                    for p in range(kq):
                        for q in range(ksplit):
                            ks = slice(q * (tk // ksplit), (q + 1) * (tk // ksplit))
                            d = _dot(xv[xi][p][:, ks], wv[wi][p][ks, :])
                            tot = d if tot is None else tot + d
                outs.append(tot)
            return outs

        def finalize(vals):
            ev = []
            for r in e_refs:
                v = r[...]
                ev.append(v.reshape(v.shape[-2:]))
            outs = epi(vals, ev)
            for o_ref, o in zip(o_refs, outs):
                o_ref[...] = o.astype(o_ref.dtype).reshape(o_ref.shape)

        if nk == 1:
            finalize(accumulate(None))
        else:
            @pl.when(k == 0)
            def _():
                for a_ref, p in zip(acc_refs, accumulate(None)):
                    a_ref[...] = p

            if nk > 2:
                @pl.when(jnp.logical_and(k > 0, k < nk - 1))
                def _():
                    for a_ref, p in zip(acc_refs, accumulate([a_ref[...] for a_ref in acc_refs])):
                        a_ref[...] = p

            @pl.when(k == nk - 1)
            def _():
                finalize(accumulate([a_ref[...] for a_ref in acc_refs]))

    scratch = [pltpu.VMEM((tm, tn), F32) for _ in range(na)] if nk > 1 else []
    args = []
    for x in xs:
        args += [x] * kq
    for w in ws:
        args += [w["arr"]] * kq
    args += [e[0] for e in extras]
    outs = pl.pallas_call(
        kern, out_shape=out_shapes, grid=grid, in_specs=in_specs, out_specs=out_specs,
        scratch_shapes=scratch, name=name,
        compiler_params=_params(("parallel", "parallel", "arbitrary")),
    )(*args)
    return outs


def _ex_row(arr, tm, tn, coff=0):
    return (arr, (tm, tn), lambda i, j, k: (i, coff + j))


def _ex_col(arr, tn, coff=0):
    return (arr, (1, tn), lambda i, j, k: (0, coff + j))


class _Mod:
    def __init__(self, ada, d, per_row, row0, t_len):
        self.d, self.per_row, self.row0, self.t_len = d, per_row, row0, t_len
        rows = ada.shape[0]
        self.a2 = ada
        self.a3 = ada.reshape(rows, 1, ada.shape[1])
        self.a3r = ada.reshape(1, rows, ada.shape[1])

    def norm_operand(self, sub, kind, tt):
        slab = sub * 3 + kind
        if self.per_row:
            return self.a3r, (1, tt, self.d), (lambda b, t: (0, t, slab))
        row0 = self.row0
        return self.a3, (1, 1, self.d), (lambda b, t: (row0 + b, 0, slab))

    def mm_extra(self, sub, kind, tm, tn):
        cb = (sub * 3 + kind) * (self.d // tn)
        if self.per_row:
            return (self.a2, (tm, tn), lambda i, j, k: (i, cb + j))
        row0, t_len = self.row0, self.t_len
        return (self.a3, (1, 1, tn), lambda i, j, k: (row0 + (i * tm) // t_len, 0, cb + j))


def _adanorm(x3, g, mod, sub, out_dtype, tt, name):
    b, t, d = x3.shape
    use_mod = mod is not None

    def kern(*refs):
        if use_mod:
            x_ref, g_ref, sh_ref, sc_ref, o_ref = refs
        else:
            x_ref, g_ref, o_ref = refs
        xf = x_ref[0]
        y = xf * lax.rsqrt(jnp.mean(xf * xf, axis=-1, keepdims=True) + RMS_EPS) * g_ref[...]
        if use_mod:
            y = y * (1.0 + sc_ref[0]) + sh_ref[0]
        o_ref[0] = y.astype(o_ref.dtype)

    in_specs = [pl.BlockSpec((1, tt, d), lambda bi, ti: (bi, ti, 0)),
                pl.BlockSpec((1, d), lambda bi, ti: (0, 0))]
    args = [x3, g.reshape(1, d)]
    if use_mod:
        for kind in (0, 1):
            arr, bshape, imap = mod.norm_operand(sub, kind, tt)
            in_specs.append(pl.BlockSpec(bshape, imap))
            args.append(arr)
    return pl.pallas_call(
        kern, out_shape=jax.ShapeDtypeStruct((b, t, d), out_dtype), grid=(b, t // tt),
        in_specs=in_specs, out_specs=pl.BlockSpec((1, tt, d), lambda bi, ti: (bi, ti, 0)),
        name=name, compiler_params=_params(("parallel", "parallel")),
    )(*args)


def _rope_tables(pos, half):
    inv_freq = ROPE_THETA ** (-jnp.arange(half, dtype=F32) / half)
    ang = pos.astype(F32)[:, None] * inv_freq[None, :]
    cos, sin = jnp.cos(ang), jnp.sin(ang)
    return jnp.concatenate([cos, cos, -sin, sin], axis=-1)


def _rope_cols(w, half):
    return jnp.concatenate([w, w[..., half:], w[..., :half]], axis=-1)


def _kvprep(a, kr_raw, tab, q_norm_g, kv_norm_g, ql, kvl, rd, tm, name):
    m = a.shape[0]

    def kern(a_ref, kr_ref, tab_ref, qg_ref, kg_ref, qn_ref, ckv_ref, kro_ref, kcat_ref):
        av = a_ref[...]
        cq, ck = av[:, :ql], av[:, ql:]
        qn = cq * lax.rsqrt(jnp.mean(cq * cq, axis=-1, keepdims=True) + RMS_EPS) * qg_ref[...]
        qn_ref[...] = qn.astype(BF16)
        ckv = ck * lax.rsqrt(jnp.mean(ck * ck, axis=-1, keepdims=True) + RMS_EPS) * kg_ref[...]
        ckv_ref[...] = ckv
        r = kr_ref[...] * tab_ref[...]
        rope = r + pltpu.roll(r, rd, 1)
        kro_ref[...] = rope[:, :rd]
        lane = lax.broadcasted_iota(jnp.int32, rope.shape, 1)
        kcat_ref[...] = jnp.concatenate([ckv, jnp.where(lane < rd, rope, 0.0)], axis=1).astype(BF16)

    return pl.pallas_call(
        kern,
        out_shape=[jax.ShapeDtypeStruct((m, ql), BF16), jax.ShapeDtypeStruct((m, kvl), F32),
                   jax.ShapeDtypeStruct((m, rd), F32), jax.ShapeDtypeStruct((m, kvl + 2 * rd), BF16)],
        grid=(m // tm,),
        in_specs=[pl.BlockSpec((tm, ql + kvl), lambda i: (i, 0)),
                  pl.BlockSpec((tm, 2 * rd), lambda i: (i, 0)),
                  pl.BlockSpec((tm, 2 * rd), lambda i: (i, 0)),
                  pl.BlockSpec((1, ql), lambda i: (0, 0)),
                  pl.BlockSpec((1, kvl), lambda i: (0, 0))],
        out_specs=[pl.BlockSpec((tm, ql), lambda i: (i, 0)), pl.BlockSpec((tm, kvl), lambda i: (i, 0)),
                   pl.BlockSpec((tm, rd), lambda i: (i, 0)), pl.BlockSpec((tm, kvl + 2 * rd), lambda i: (i, 0))],
        name=name, compiler_params=_params(("parallel",)),
    )(a, kr_raw, tab, q_norm_g.reshape(1, ql), kv_norm_g.reshape(1, kvl))


def _qproj(qn, w_q, w_uk_t, tab, tm, name):
    m, ql = qn.shape
    nh, nd, kvl = w_uk_t.shape
    rd2 = w_q.shape[2] - nd

    def kern(qn_ref, wq_ref, wuk_ref, tab_ref, o_ref):
        qq = _dot(qn_ref[...], wq_ref[...].astype(BF16))
        q_lat = _dot(qq[:, :nd].astype(BF16), wuk_ref[...].astype(BF16))
        r = qq[:, nd:] * tab_ref[...]
        q_rope = r + pltpu.roll(r, rd2 // 2, 1)
        o_ref[...] = jnp.concatenate([q_lat, q_rope], axis=1).astype(BF16)

    return pl.pallas_call(
        kern, out_shape=jax.ShapeDtypeStruct((nh, m, kvl + rd2), BF16), grid=(nh, m // tm),
        in_specs=[pl.BlockSpec((tm, ql), lambda h, i: (i, 0)),
                  pl.BlockSpec((None, ql, nd + rd2), lambda h, i: (h, 0, 0)),
                  pl.BlockSpec((None, nd, kvl), lambda h, i: (h, 0, 0)),
                  pl.BlockSpec((tm, rd2), lambda h, i: (i, 0))],
        out_specs=pl.BlockSpec((None, tm, kvl + rd2), lambda h, i: (h, i, 0)),
        name=name, compiler_params=_params(("parallel", "parallel")),
    )(qn, w_q, w_uk_t, tab)


def _prompt_attention(qcat, kcat, w_uv, n_seq, t_len, scale, tq, hb, name):
    nh, m, e = qcat.shape
    _, kvl, vd = w_uv.shape
    nq = t_len // tq
    heads = range(hb)

    def kern(q_ref, k_ref, w_ref, o_ref, m_ref, l_ref, acc_ref, s_ref):
        qi = pl.program_id(2)
        m_ref[...] = jnp.full(m_ref.shape, -jnp.inf, F32)
        l_ref[...] = jnp.zeros(l_ref.shape, F32)
        acc_ref[...] = jnp.zeros(acc_ref.shape, F32)

        def scores(h, j):
            kb = k_ref[pl.ds(pl.multiple_of(j * tq, tq), tq), :]
            return _dot_nt(q_ref[h], kb) * scale

        def soft_pv(h, s, j):
            m_prev = m_ref[h]
            m_new = jnp.maximum(m_prev, jnp.max(s, axis=-1, keepdims=True))
            alpha = jnp.exp(m_prev - m_new)
            p = jnp.exp(s - m_new)
            l_ref[h] = l_ref[h] * alpha + jnp.sum(p, axis=-1, keepdims=True)
            m_ref[h] = m_new
            vb = k_ref[pl.ds(pl.multiple_of(j * tq, tq), tq), :kvl]
            acc_ref[h] = acc_ref[h] * alpha + _dot(p.astype(BF16), vb)

        for h in heads:
            s_ref[h] = scores(h, 0)

        def body(j, carry):
            ahead = min(2, hb)
            nxt = {h: scores(h, j + 1) for h in range(ahead)}
            for h in heads:
                s_cur = s_ref[h]
                s_ref[h] = nxt.pop(h)
                soft_pv(h, s_cur, j)
                if h + ahead < hb:
                    nxt[h + ahead] = scores(h + ahead, j + 1)
            return carry

        lax.fori_loop(0, qi, body, 0)
        row = lax.broadcasted_iota(jnp.int32, (tq, tq), 0)
        col = lax.broadcasted_iota(jnp.int32, (tq, tq), 1)
        for h in heads:
            soft_pv(h, jnp.where(col <= row, s_ref[h], -jnp.inf), qi)
        for h in heads:
            o_lat = (acc_ref[h] / l_ref[h]).astype(BF16)
            o_ref[:, h * vd:(h + 1) * vd] = _dot(o_lat, w_ref[h].astype(BF16)).astype(BF16)

    return pl.pallas_call(
        kern, out_shape=jax.ShapeDtypeStruct((m, nh * vd), BF16), grid=(n_seq, nh // hb, nq),
        in_specs=[pl.BlockSpec((hb, tq, e), lambda b, hg, qi: (hg, b * nq + qi, 0)),
                  pl.BlockSpec((t_len, e), lambda b, hg, qi: (b, 0)),
                  pl.BlockSpec((hb, kvl, vd), lambda b, hg, qi: (hg, 0, 0))],
        out_specs=pl.BlockSpec((tq, hb * vd), lambda b, hg, qi: (b * nq + qi, hg)),
        scratch_shapes=[pltpu.VMEM((hb, tq, 1), F32), pltpu.VMEM((hb, tq, 1), F32),
                        pltpu.VMEM((hb, tq, kvl), F32), pltpu.VMEM((hb, tq, tq), F32)],
        name=name, compiler_params=_params(("parallel", "parallel", "arbitrary")),
    )(qcat, kcat, w_uv)


def _decode_attention(q, cache_kv, cache_kr_t, layer, page_table, ckv_new, kr_new, scale, pp, name):
    bsz, nh, e = q.shape
    page, kvl = cache_kv.shape[2], cache_kv.shape[3]
    rd = cache_kr_t.shape[2]
    n_pages = page_table.shape[1]
    assert n_pages % pp == 0
    ns = n_pages // pp

    def kern(pt_ref, q_ref, *refs):
        kv_refs = refs[:pp]
        kr_refs = refs[pp:2 * pp]
        cn_ref, rn_ref, o_ref, m_ref, l_ref, acc_ref, kvb_ref, krb_ref = refs[2 * pp:]
        s_idx = pl.program_id(1)

        @pl.when(s_idx == 0)
        def _():
            m_ref[...] = jnp.full(m_ref.shape, -jnp.inf, F32)
            l_ref[...] = jnp.zeros(l_ref.shape, F32)
            acc_ref[...] = jnp.zeros(acc_ref.shape, F32)

        qv = q_ref[0]
        ql, qr = qv[:, :kvl], qv[:, kvl:kvl + rd]

        def online(s, pv_fn):
            m_prev = m_ref[...]
            m_new = jnp.maximum(m_prev, jnp.max(s, axis=-1, keepdims=True))
            alpha = jnp.exp(m_prev - m_new)
            p = jnp.exp(s - m_new)
            l_ref[...] = l_ref[...] * alpha + jnp.sum(p, axis=-1, keepdims=True)
            acc_ref[...] = acc_ref[...] * alpha + pv_fn(p)
            m_ref[...] = m_new

        for j in range(pp):
            kvb_ref[j * page:(j + 1) * page, :] = kv_refs[j][...].astype(BF16)
            krb_ref[:, j * page:(j + 1) * page] = kr_refs[j][...].astype(BF16)
        kv_all = kvb_ref[...]
        s = (_dot_nt(ql, kv_all) + _dot(qr, krb_ref[...])) * scale
        online(s, lambda p: _dot(p.astype(BF16), kv_all))

        @pl.when(s_idx == ns - 1)
        def _():
            cn = cn_ref[0].astype(BF16).astype(F32)
            rn = rn_ref[0].astype(BF16).astype(F32)
            s_new = (jnp.sum(ql.astype(F32) * cn, axis=-1, keepdims=True)
                     + jnp.sum(qr.astype(F32) * rn, axis=-1, keepdims=True)) * scale
            online(s_new, lambda p: p.astype(BF16).astype(F32) * cn)
            o_ref[0] = acc_ref[...] / l_ref[...]

    in_specs = [pl.BlockSpec((1, nh, e), lambda b, s, pt: (b, 0, 0))]
    for j in range(pp):
        in_specs.append(pl.BlockSpec((None, None, page, kvl),
                                     lambda b, s, pt, j=j: (layer, pt[b, s * pp + j], 0, 0)))
    for j in range(pp):
        in_specs.append(pl.BlockSpec((None, None, rd, page),
                                     lambda b, s, pt, j=j: (layer, pt[b, s * pp + j], 0, 0)))
    in_specs.append(pl.BlockSpec((1, 1, kvl), lambda b, s, pt: (b, 0, 0)))
    in_specs.append(pl.BlockSpec((1, 1, rd), lambda b, s, pt: (b, 0, 0)))
    grid_spec = pltpu.PrefetchScalarGridSpec(
        num_scalar_prefetch=1, grid=(bsz, ns), in_specs=in_specs,
        out_specs=pl.BlockSpec((1, nh, kvl), lambda b, s, pt: (b, 0, 0)),
        scratch_shapes=[pltpu.VMEM((nh, 1), F32), pltpu.VMEM((nh, 1), F32), pltpu.VMEM((nh, kvl), F32),
                        pltpu.VMEM((pp * page, kvl), BF16), pltpu.VMEM((rd, pp * page), BF16)])
    return pl.pallas_call(
        kern, out_shape=jax.ShapeDtypeStruct((bsz, nh, kvl), F32), grid_spec=grid_spec, name=name,
        compiler_params=_params(("parallel", "arbitrary")),
    )(page_table, q, *([cache_kv] * pp), *([cache_kr_t] * pp), ckv_new, kr_new)


def _oproj(o_lat, w_uv, tm, name):
    nh, m, kvl = o_lat.shape
    vd = w_uv.shape[2]

    def kern(o_ref, w_ref, y_ref):
        y_ref[...] = _dot(o_ref[...].astype(BF16), w_ref[...].astype(BF16)).astype(BF16)

    return pl.pallas_call(
        kern, out_shape=jax.ShapeDtypeStruct((m, nh * vd), BF16), grid=(nh, m // tm),
        in_specs=[pl.BlockSpec((None, tm, kvl), lambda h, i: (h, i, 0)),
                  pl.BlockSpec((None, kvl, vd), lambda h, i: (h, 0, 0))],
        out_specs=pl.BlockSpec((tm, vd), lambda h, i: (i, h)),
        name=name, compiler_params=_params(("parallel", "parallel")),
    )(o_lat, w_uv)


def _conv_prompt(bcu, conv_w, n_seq, t_len, cd, tc, name):
    m = bcu.shape[0]
    ncb = cd // tc
    cw = conv_w.shape[0]

    def kern(b_ref, c_ref, u_ref, w_ref, y_ref, st_ref):
        xg = c_ref[...] * u_ref[...]
        w = w_ref[...]
        row = lax.broadcasted_iota(jnp.int32, xg.shape, 0)
        y = xg * w[cw - 1:cw]
        for d in range(1, cw):
            sh = jnp.where(row >= d, pltpu.roll(xg, d, 0), 0.0)
            y = y + sh * w[cw - 1 - d:cw - d]
        y_ref[...] = (b_ref[...] * y).astype(BF16)
        st_ref[0] = xg[t_len - (cw - 1):, :]

    return pl.pallas_call(
        kern, out_shape=[jax.ShapeDtypeStruct((m, cd), BF16), jax.ShapeDtypeStruct((n_seq, cw - 1, cd), F32)],
        grid=(n_seq, ncb),
        in_specs=[pl.BlockSpec((t_len, tc), lambda b, j: (b, j)),
                  pl.BlockSpec((t_len, tc), lambda b, j: (b, ncb + j)),
                  pl.BlockSpec((t_len, tc), lambda b, j: (b, 2 * ncb + j)),
                  pl.BlockSpec((cw, tc), lambda b, j: (0, j))],
        out_specs=[pl.BlockSpec((t_len, tc), lambda b, j: (b, j)),
                   pl.BlockSpec((1, cw - 1, tc), lambda b, j: (b, 0, j))],
        name=name, compiler_params=_params(("parallel", "parallel")),
    )(bcu, bcu, bcu, conv_w)


def _conv_step(bcu, prev, conv_w, cd, tc, name):
    bsz = bcu.shape[0]
    ncb = cd // tc
    cw = conv_w.shape[0]

    def kern(*refs):
        b_ref, c_ref, u_ref, w_ref = refs[:4]
        p_refs = refs[4:4 + cw - 1]
        y_ref = refs[4 + cw - 1]
        st_refs = refs[5 + cw - 1:]
        xg = c_ref[...] * u_ref[...]
        w = w_ref[...]
        y = xg * w[cw - 1:cw]
        for d in range(cw - 1):
            y = y + p_refs[d][...] * w[d:d + 1]
        y_ref[...] = (b_ref[...] * y).astype(BF16)
        for d in range(cw - 2):
            st_refs[d][...] = p_refs[d + 1][...]
        st_refs[cw - 2][...] = xg

    in_specs = [pl.BlockSpec((bsz, tc), lambda j: (0, j)),
                pl.BlockSpec((bsz, tc), lambda j: (0, ncb + j)),
                pl.BlockSpec((bsz, tc), lambda j: (0, 2 * ncb + j)),
                pl.BlockSpec((cw, tc), lambda j: (0, j))]
    for d in range(cw - 1):
        in_specs.append(pl.BlockSpec((bsz, tc), lambda j, d=d: (0, d * ncb + j)))
    outs = pl.pallas_call(
        kern,
        out_shape=[jax.ShapeDtypeStruct((bsz, cd), BF16)] + [jax.ShapeDtypeStruct((bsz, cd), F32)] * (cw - 1),
        grid=(ncb,), in_specs=in_specs,
        out_specs=[pl.BlockSpec((bsz, tc), lambda j: (0, j))] * cw,
        name=name, compiler_params=_params(("parallel",)),
    )(bcu, bcu, bcu, conv_w, *([prev] * (cw - 1)))
    return outs[0], jnp.stack(outs[1:], axis=1)


def _token_mix(h3, prev3, mu, tc, name):
    b, t, d = h3.shape
    nmix = mu.shape[0]
    shift_here = prev3 is None

    def kern(*refs):
        if shift_here:
            h_ref, mu_ref = refs[:2]
            o_refs = refs[2:]
        else:
            h_ref, p_ref, mu_ref = refs[:3]
            o_refs = refs[3:]
        hh = h_ref[0]
        if shift_here:
            row = lax.broadcasted_iota(jnp.int32, hh.shape, 0)
            prev = jnp.where(row >= 1, pltpu.roll(hh, 1, 0), 0.0)
        else:
            prev = p_ref[0]
        xx = prev - hh
        mv = mu_ref[...]
        for i in range(nmix):
            o_refs[i][0] = (hh + xx * mv[i:i + 1]).astype(BF16)

    blk = pl.BlockSpec((1, t, tc), lambda bi, j: (bi, 0, j))
    in_specs = [blk] + ([] if shift_here else [blk]) + [pl.BlockSpec((nmix, tc), lambda bi, j: (0, j))]
    args = [h3] + ([] if shift_here else [prev3]) + [mu]
    return pl.pallas_call(
        kern, out_shape=[jax.ShapeDtypeStruct((b, t, d), BF16)] * nmix, grid=(b, d // tc),
        in_specs=in_specs, out_specs=[blk] * nmix,
        name=name, compiler_params=_params(("parallel", "parallel")),
    )(*args)


def _rowsum_mxu(xs, ones):
    his = [x.astype(BF16) for x in xs]
    los = [(x - hi.astype(F32)).astype(BF16) for x, hi in zip(xs, his)]
    return [_dot(hi, ones) + _dot(lo, ones) for hi, lo in zip(his, los)]


def _wkv_prompt(r, k, v, a, lw, g, k_k, k_a, r_k, ln_w, ln_b, n_seq, t_len, hs, name):
    m, d = r.shape
    c = WKV_CHUNK
    hl = min(WKV_LANES, d)
    hpb = hl // hs
    nh = d // hs
    nc = t_len // c

    def kern(r_ref, k_ref, v_ref, a_ref, lw_ref, g_ref, kk_ref, ka_ref, rk_ref, lnw_ref, lnb_ref,
             o_ref, s_ref):
        ci = pl.program_id(2)

        @pl.when(ci == 0)
        def _():
            s_ref[...] = jnp.zeros(s_ref.shape, F32)

        ri = lax.broadcasted_iota(jnp.int32, (c, c), 0)
        cj = lax.broadcasted_iota(jnp.int32, (c, c), 1)
        strict = cj < ri
        ri2 = lax.broadcasted_iota(jnp.int32, (c, 2 * c), 0)
        cj2 = lax.broadcasted_iota(jnp.int32, (c, 2 * c), 1)
        incl2 = jnp.where(cj2 >= c, cj2 - c, cj2) <= ri2
        tri = jnp.where(cj <= ri, 1.0, 0.0).astype(BF16)
        eye = jnp.where(cj == ri, 1.0, 0.0).astype(F32)
        heads = range(hpb)
        sls = [slice(hh * hs, (hh + 1) * hs) for hh in heads]

        lw_all = lw_ref[...]
        lw_hi = lw_all.astype(BF16)
        lw_lo = (lw_all - lw_hi.astype(F32)).astype(BF16)
        cum_all = _dot(tri, lw_hi) + _dot(tri, lw_lo)
        ones = jnp.ones((hs, hs), BF16)
        kk0 = [k_ref[:, sl] * kk_ref[:, sl] for sl in sls]
        ssq = _rowsum_mxu([x * x for x in kk0], ones)
        at, rt, bk_t, bk_h, vb, kmods, gtot = [], [], [], [], [], [], []
        for hh in heads:
            sl = sls[hh]
            rr, k0, aa, lwh = r_ref[:, sl], k_ref[:, sl], a_ref[:, sl], lw_all[:, sl]
            kk = kk0[hh] * lax.rsqrt(jnp.maximum(ssq[hh], 1e-24))
            kmod = k0 * (1.0 + (aa - 1.0) * ka_ref[:, sl])
            avec, bvec = -kk, kk * aa
            cum = cum_all[:, sl]
            tot = cum[c - 1:c, :]
            at.append((avec * jnp.exp(cum - lwh)).astype(BF16))
            rt.append((rr * jnp.exp(cum)).astype(BF16))
            e_neg = jnp.exp(-cum)
            e_tot = jnp.exp(tot - cum)
            bk_t.append(jnp.concatenate([bvec * e_neg, kmod * e_neg], axis=0).astype(BF16))
            bk_h.append(jnp.concatenate([bvec * e_tot, kmod * e_tot], axis=0).astype(BF16))
            vb.append(v_ref[:, sl].astype(BF16))
            kmods.append(kmod)
            gtot.append(jnp.exp(tot))
        s0 = [s_ref[0, hh] for hh in heads]
        sc = [_dot_nt(jnp.concatenate([at[hh], rt[hh]], axis=0),
                      jnp.concatenate([bk_t[hh], s0[hh].astype(BF16)], axis=0)) for hh in heads]
        a_ab = [jnp.where(strict, sc[hh][:c, :c], 0.0) for hh in heads]
        a_ak = [jnp.where(strict, sc[hh][:c, c:2 * c], 0.0).astype(BF16) for hh in heads]
        a_r = [jnp.where(incl2, sc[hh][c:, :2 * c], 0.0).astype(BF16) for hh in heads]
        ws = [sc[hh][:, 2 * c:] for hh in heads]
        akv = [_dot(a_ak[hh], vb[hh]) for hh in heads]
        tinv = [eye + a_ab[hh] for hh in heads]
        lp = [a_ab[hh].astype(BF16) for hh in heads]
        n_done = 2
        while n_done < c:
            lp = [_dot(lp[hh], lp[hh]).astype(BF16) for hh in heads]
            tinv = [tinv[hh] + _dot(tinv[hh].astype(BF16), lp[hh]) for hh in heads]
            n_done *= 2
        p = [_dot(tinv[hh].astype(BF16), (ws[hh][:c] + akv[hh]).astype(BF16)) for hh in heads]
        z = [jnp.concatenate([p[hh].astype(BF16), vb[hh]], axis=0) for hh in heads]
        y = [ws[hh][c:] + _dot(a_r[hh], z[hh]) for hh in heads]
        s1 = [s0[hh] * gtot[hh] + _dot_tn(z[hh], bk_h[hh]) for hh in heads]
        rs = _rowsum_mxu([jnp.concatenate([y[hh], y[hh] * y[hh], r_ref[:, sls[hh]] * kmods[hh] * rk_ref[:, sls[hh]]],
                                          axis=0) for hh in heads], ones)
        for hh in heads:
            sl = sls[hh]
            s_ref[0, hh] = s1[hh]
            mean = rs[hh][:c] * (1.0 / hs)
            var = jnp.maximum(rs[hh][c:2 * c] * (1.0 / hs) - mean * mean, 0.0)
            yn = (y[hh] - mean) * lax.rsqrt(var + GN_EPS) * lnw_ref[:, sl] + lnb_ref[:, sl]
            o_ref[:, sl] = ((yn + rs[hh][2 * c:] * v_ref[:, sl]) * g_ref[:, sl]).astype(BF16)

    blk = pl.BlockSpec((c, hl), lambda b, hg, ci: (b * nc + ci, hg))
    wblk = pl.BlockSpec((1, hl), lambda b, hg, ci: (0, hg))
    return pl.pallas_call(
        kern, out_shape=[jax.ShapeDtypeStruct((m, d), BF16), jax.ShapeDtypeStruct((n_seq, nh, hs, hs), F32)],
        grid=(n_seq, d // hl, nc),
        in_specs=[blk] * 6 + [wblk] * 5,
        out_specs=[blk, pl.BlockSpec((1, hpb, hs, hs), lambda b, hg, ci: (b, hg, 0, 0))],
        name=name, compiler_params=_params(("parallel", "parallel", "arbitrary")),
    )(r, k, v, a, lw, g, k_k, k_a, r_k, ln_w, ln_b)


def _wkv_step(r, k, a, lw, v, g, s0, k_k, k_a, r_k, ln_w, ln_b, hb, name):
    nh, hs, bsz = r.shape

    def kern(r_ref, k_ref, a_ref, lw_ref, v_ref, g_ref, s_ref, kk_ref, ka_ref, rk_ref, lnw_ref, lnb_ref,
             o_ref, so_ref):
        rr, k0, aa = r_ref[...], k_ref[...], a_ref[...]
        vv = v_ref[...]
        dec = jnp.exp(lw_ref[...])
        kk = k0 * kk_ref[...]
        kk = kk * lax.rsqrt(jnp.maximum(jnp.sum(kk * kk, axis=1, keepdims=True), 1e-24))
        kmod = k0 * (1.0 + (aa - 1.0) * ka_ref[...])
        avec, bvec = -kk, kk * aa
        s = s_ref[...]
        sa = jnp.sum(s * avec[:, None], axis=2, keepdims=True)
        s1 = s * dec[:, None] + sa * bvec[:, None] + vv * kmod[:, None]
        so_ref[...] = s1
        y = jnp.sum(s1 * rr[:, None], axis=2, keepdims=True)
        mean = jnp.mean(y, axis=1, keepdims=True)
        var = jnp.mean(jnp.square(y - mean), axis=1, keepdims=True)
        yn = (y - mean) * lax.rsqrt(var + GN_EPS) * lnw_ref[...] + lnb_ref[...]
        bonus = jnp.sum(rr * kmod * rk_ref[...], axis=1, keepdims=True)[:, None] * vv
        o_ref[...] = (yn + bonus) * g_ref[...]

    kblk = pl.BlockSpec((hb, hs, bsz), lambda i: (i, 0, 0))
    vblk = pl.BlockSpec((hb, hs, 1, bsz), lambda i: (i, 0, 0, 0))
    sblk = pl.BlockSpec((hb, hs, hs, bsz), lambda i: (i, 0, 0, 0))
    wkblk = pl.BlockSpec((hb, hs, 1), lambda i: (i, 0, 0))
    wvblk = pl.BlockSpec((hb, hs, 1, 1), lambda i: (i, 0, 0, 0))
    return pl.pallas_call(
        kern, out_shape=[jax.ShapeDtypeStruct((nh, hs, 1, bsz), F32), jax.ShapeDtypeStruct((nh, hs, hs, bsz), F32)],
        grid=(nh // hb,),
        in_specs=[kblk] * 4 + [vblk] * 2 + [sblk] + [wkblk] * 3 + [wvblk] * 2,
        out_specs=[vblk, sblk],
        name=name, compiler_params=_params(("parallel",)),
    )(r, k, a, lw, v, g, s0, k_k, k_a, r_k, ln_w, ln_b)


def _pick(n, candidates):
    for c in candidates:
        if n % c == 0:
            return c
    return n


def kernel(x_prompt, x_sample, cache_kv_latent, cache_k_rope, state_conv, state_rwkv_shift, state_rwkv_wkv, page_table, c_prompt, c_sample, w_ada, b_ada, norm_g, ffn_w1, ffn_w3, ffn_w2, mla_w_in, mla_q_norm, mla_kv_norm, mla_w_uq, mla_w_ukv, conv_w, mix_w_out, rwkv_mu, rwkv_w0, rwkv_w1, rwkv_w2, rwkv_a0, rwkv_a1, rwkv_a2, rwkv_g1, rwkv_g2, rwkv_k_k, rwkv_k_a, rwkv_r_k, rwkv_w_r, rwkv_w_k, rwkv_w_v, rwkv_w_o, rwkv_ln_w, rwkv_ln_b, final_norm_g):
    nb, seq, d = x_prompt.shape
    nsb, dec_seq, _ = x_sample.shape
    assert dec_seq == 1
    depth = w_ada.shape[0]
    dff = ffn_w1.shape[-1]
    ql = mla_q_norm.shape[1]
    kvl = mla_kv_norm.shape[1]
    rd = cache_k_rope.shape[-1]
    nh_mla = mla_w_ukv.shape[2]
    nd = mla_w_uq.shape[2] // nh_mla - rd
    vd = mla_w_ukv.shape[3] - nd
    cd = conv_w.shape[-1]
    cw = conv_w.shape[1]
    hs = state_rwkv_wkv.shape[-1]
    nh_rwkv = d // hs
    past_len = page_table.shape[1] * cache_kv_latent.shape[2]
    mla_scale = float((nd + rd) ** -0.5)
    mp = nb * seq

    n_c = nsb + nb
    n_c_pad = -(-n_c // 16) * 16
    c_all = jnp.concatenate([c_sample, c_prompt, jnp.zeros((n_c_pad - n_c, d), F32)], axis=0)
    t_ada = _pick(9 * d, (512, 256, 128))
    adas = []
    for layer in range(depth):
        (ada,) = _mm([c_all], [_w(w_ada, (layer,))], [[(0, 0)]],
                     lambda acc, ev: (acc[0] + ev[0],), [F32],
                     n=9 * d, tm=n_c_pad, tn=t_ada, x_pro=_silu,
                     extras=[_ex_col(b_ada[layer].reshape(1, 9 * d), t_ada)], name=f"ada_l{layer}")
        adas.append(ada)

    ffn_bf16 = {}
    w_bf16 = {}

    def run_group(x2, is_sample):
        m = x2.shape[0]
        if is_sample:
            n_seq, t_len = m, 1
            xshape = (1, m, d)
            tm = m
            tt = m
            pos = jnp.full((m,), past_len, jnp.int32)
        else:
            n_seq, t_len = nb, seq
            xshape = (nb, seq, d)
            tm = _pick(seq, (1024, 512, 256, 128))
            tt = _pick(seq, (256, 128))
            pos = jnp.tile(jnp.arange(seq, dtype=jnp.int32), nb)
        tn = _pick(d, (512, 256, 128))
        tab = _rope_tables(pos, rd // 2)
        kv_rows, kr_rows, conv_states, shift_states, wkv_states = [], [], [], [], []

        def mm_shared(key, x, make_w, epi, out_dtype, **kw):
            if is_sample:
                o, wb = _mm([x], [make_w()], [[(0, 0)]], epi, [out_dtype], cast_ws=(0,), **kw)
                w_bf16[key] = wb
            else:
                (o,) = _mm([x], [_w(w_bf16[key])], [[(0, 0)]], epi, [out_dtype], **kw)
            return o

        def ffn(x2, mod, layer, idx, sub, tag):
            h = _adanorm(x2.reshape(xshape), norm_g[layer, sub], mod, sub, BF16, tt, f"norm_{tag}").reshape(m, d)
            tnf = _pick(dff, (256, 128))
            up_epi = lambda acc, ev: (_silu(acc[0]) * acc[1],)
            down_epi = lambda acc, ev: (ev[0] + FFN_RES * ev[1] * acc[0],)
            if is_sample:
                gact, w1b, w3b = _mm([h], [_w(ffn_w1, (layer, idx)), _w(ffn_w3, (layer, idx))],
                                     [[(0, 0)], [(0, 1)]], up_epi, [BF16],
                                     n=dff, tm=tm, tn=tnf, cast_ws=(0, 1), name=f"ffn_up_{tag}")
                tnd = _pick(d, (256, 128))
                xo, w2b = _mm([gact], [_w(ffn_w2, (layer, idx))], [[(0, 0)]], down_epi, [F32],
                              n=d, tm=tm, tn=tnd, cast_ws=(0,),
                              extras=[_ex_row(x2, tm, tnd), mod.mm_extra(sub, 2, tm, tnd)], name=f"ffn_down_{tag}")
                ffn_bf16[(layer, idx)] = (w1b, w3b, w2b)
            else:
                w1b, w3b, w2b = ffn_bf16[(layer, idx)]
                (gact,) = _mm([h], [_w(w1b), _w(w3b)], [[(0, 0)], [(0, 1)]], up_epi, [BF16],
                              n=dff, tm=_pick(seq, (2048, 1024, 512, 256, 128)), tn=tnf, x_resident=True,
                              name=f"ffn_up_{tag}")
                tnd = _pick(d, (256, 128))
                (xo,) = _mm([gact], [_w(w2b)], [[(0, 0)]], down_epi, [F32],
                            n=d, tm=tm, tn=tnd, x_resident=True, ksplit=2 if dff % 256 == 0 else 1,
                            extras=[_ex_row(x2, tm, tnd), mod.mm_extra(sub, 2, tm, tnd)], name=f"ffn_down_{tag}")
            return xo

        for layer in range(depth):
            tag = f"{'s' if is_sample else 'p'}{layer}"
            mod = _Mod(adas[layer], d, is_sample, nsb, t_len)
            x2 = ffn(x2, mod, layer, 0, 0, tag + "a")
            if layer % 2 == 0:
                i = layer // 2
                h = _adanorm(x2.reshape(xshape), norm_g[layer, 1], mod, 1, BF16, tt, f"norm_mix_{tag}").reshape(m, d)
                n_a = ql + kvl
                (a_proj,) = _mm([h], [_w(mla_w_in, (i,))], [[(0, 0)]], lambda acc, ev: (acc[0],), [F32],
                                n=n_a, tm=tm, tn=_pick(n_a, (512, 256, 128)), name=f"mla_in_a_{tag}")
                w_kr = _rope_cols(mla_w_in[i, :, n_a:n_a + rd], rd // 2)
                (kr_raw,) = _mm([h], [_w(w_kr)], [[(0, 0)]], lambda acc, ev: (acc[0],), [F32],
                                n=2 * rd, tm=tm, tn=2 * rd, name=f"mla_in_kr_{tag}")
                bcu = mm_shared(("bcu", i), h, lambda: _w(mla_w_in[i, :, n_a + rd:]), lambda acc, ev: (acc[0],), F32,
                                n=3 * cd, tm=tm, tn=_pick(cd, (512, 256, 128)), name=f"mla_in_bcu_{tag}")
                qn, ckv, kr, kcat = _kvprep(a_proj, kr_raw, tab, mla_q_norm[i], mla_kv_norm[i], ql, kvl, rd,
                                            _pick(m, (256, 128)), f"kvprep_{tag}")
                w_uq = mla_w_uq[i].reshape(ql, nh_mla, nd + rd)
                w_uq_h = jnp.transpose(w_uq, (1, 0, 2))
                w_q = jnp.concatenate([w_uq_h[:, :, :nd], _rope_cols(w_uq_h[:, :, nd:], rd // 2)], axis=-1)
                w_uk_t = jnp.transpose(mla_w_ukv[i][:, :, :nd], (1, 2, 0))
                w_uv = jnp.transpose(mla_w_ukv[i][:, :, nd:], (1, 0, 2))
                qcat = _qproj(qn, w_q, w_uk_t, tab, _pick(m, (1024, 512, 256, 128)), f"qproj_{tag}")
                if is_sample:
                    o_lat = _decode_attention(jnp.transpose(qcat, (1, 0, 2)), cache_kv_latent,
                                              jnp.swapaxes(cache_k_rope, 2, 3), i,
                                              page_table, ckv.reshape(m, 1, kvl), kr.reshape(m, 1, rd),
                                              mla_scale, _pick(page_table.shape[1], (32, 16, 8, 4, 2, 1)), f"attn_{tag}")
                    o_att = _oproj(jnp.transpose(o_lat, (1, 0, 2)), w_uv, _pick(m, (512, 256, 128)), f"oproj_{tag}")
                    prev = state_conv[i].reshape(m, (cw - 1) * cd)
                    conv_y, conv_new = _conv_step(bcu, prev, conv_w[i], cd, _pick(cd, (512, 256, 128)), f"conv_{tag}")
                else:
                    o_att = _prompt_attention(qcat, kcat, w_uv, n_seq, t_len, mla_scale,
                                              _pick(t_len, (ATTN_TQ, 128)), _pick(nh_mla, (ATTN_HEADS, 2, 1)),
                                              f"attn_{tag}")
                    conv_y, conv_new = _conv_prompt(bcu, conv_w[i], n_seq, t_len, cd, _pick(cd, (256, 128)),
                                                    f"conv_{tag}")
                assert nh_mla * vd == cd
                (x2,) = _mm([o_att, conv_y], [_w(mix_w_out, (i,)), _w(mix_w_out, (i,), roff=1)], [[(0, 0), (1, 1)]],
                            lambda acc, ev: (ev[0] + ev[1] * acc[0],), [F32],
                            n=d, tm=tm, tn=tn, extras=[_ex_row(x2, tm, tn), mod.mm_extra(1, 2, tm, tn)],
                            name=f"mix_out_{tag}")
                kv_rows.append(ckv.reshape(n_seq, t_len, kvl))
                kr_rows.append(kr.reshape(n_seq, t_len, rd))
                conv_states.append(conv_new)
            else:
                j = layer // 2
                h3 = _adanorm(x2.reshape(xshape), norm_g[layer, 1], mod, 1, F32, tt, f"norm_mix_{tag}")
                tcm = _pick(d, (256, 128))
                if is_sample:
                    mixes = _token_mix(h3, state_rwkv_shift[j].reshape(xshape), rwkv_mu[j], tcm, f"tmix_{tag}")
                    shift_states.append(h3.reshape(m, d))
                else:
                    mixes = _token_mix(h3, None, rwkv_mu[j], tcm, f"tmix_{tag}")
                    shift_states.append(h3[:, -1])
                xr, xw, xk, xv, xa, xg = (t.reshape(m, d) for t in mixes)

                def proj(x, w, tag2):
                    return mm_shared((tag2, j), x, lambda: _w(w, (j,)), lambda acc, ev: (acc[0],), F32,
                                     n=d, tm=tm, tn=tn, name=f"rwkv_{tag2}_{tag}")

                r, k, v = proj(xr, rwkv_w_r, "r"), proj(xk, rwkv_w_k, "k"), proj(xv, rwkv_w_v, "v")
                dl = rwkv_w1.shape[-1]
                (t1,) = _mm([xw], [_w(rwkv_w1, (j,))], [[(0, 0)]], lambda acc, ev: (jnp.tanh(acc[0]),), [BF16],
                            n=dl, tm=tm, tn=dl, name=f"rwkv_w1_{tag}")

                def decay_epi(acc, ev):
                    z = -(ev[0] + acc[0])
                    softplus = jnp.maximum(z, 0.0) + jnp.log(1.0 + jnp.exp(-jnp.abs(z)))
                    return (-jnp.exp(-softplus - 0.5),)

                (lw,) = _mm([t1], [_w(rwkv_w2, (j,))], [[(0, 0)]], decay_epi, [F32],
                            n=d, tm=tm, tn=tn, extras=[_ex_col(rwkv_w0[j].reshape(1, d), tn)], name=f"rwkv_w2_{tag}")
                al = rwkv_a1.shape[-1]
                (t2,) = _mm([xa], [_w(rwkv_a1, (j,))], [[(0, 0)]], lambda acc, ev: (acc[0],), [BF16],
                            n=al, tm=tm, tn=al, name=f"rwkv_a1_{tag}")
                (a,) = _mm([t2], [_w(rwkv_a2, (j,))], [[(0, 0)]], lambda acc, ev: (_sigmoid(ev[0] + acc[0]),), [F32],
                           n=d, tm=tm, tn=tn, extras=[_ex_col(rwkv_a0[j].reshape(1, d), tn)], name=f"rwkv_a2_{tag}")
                gl = rwkv_g1.shape[-1]
                glp = -(-gl // 128) * 128
                g1p = jnp.pad(rwkv_g1[j], ((0, 0), (0, glp - gl)))
                g2p = jnp.pad(rwkv_g2[j], ((0, glp - gl), (0, 0)))
                (t3,) = _mm([xg], [_w(g1p)], [[(0, 0)]], lambda acc, ev: (_sigmoid(acc[0]),), [BF16],
                            n=glp, tm=tm, tn=glp, name=f"rwkv_g1_{tag}")
                (g,) = _mm([t3], [_w(g2p)], [[(0, 0)]], lambda acc, ev: (acc[0],), [F32],
                           n=d, tm=tm, tn=tn, name=f"rwkv_g2_{tag}")
                if is_sample:
                    tk3 = lambda t: t.T.reshape(nh_rwkv, hs, m)
                    tv4 = lambda t: t.T.reshape(nh_rwkv, hs, 1, m)
                    wk3 = lambda t: t.reshape(nh_rwkv, hs, 1)
                    wv4 = lambda t: t.reshape(nh_rwkv, hs, 1, 1)
                    o4, s_t = _wkv_step(tk3(r), tk3(k), tk3(a), tk3(lw), tv4(v), tv4(g),
                                        jnp.transpose(state_rwkv_wkv[j], (1, 2, 3, 0)),
                                        wk3(rwkv_k_k[j]), wk3(rwkv_k_a[j]), wk3(rwkv_r_k[j]),
                                        wv4(rwkv_ln_w[j]), wv4(rwkv_ln_b[j]),
                                        _pick(nh_rwkv, (2, 1)), f"wkv_{tag}")
                    o_pre = o4.reshape(d, m).T
                    s_new = jnp.transpose(s_t, (3, 0, 1, 2))
                else:
                    w2d = lambda t: t.reshape(1, d)
                    o_pre, s_new = _wkv_prompt(r, k, v, a, lw, g, w2d(rwkv_k_k[j]), w2d(rwkv_k_a[j]),
                                               w2d(rwkv_r_k[j]), w2d(rwkv_ln_w[j]), w2d(rwkv_ln_b[j]),
                                               n_seq, t_len, hs, f"wkv_{tag}")
                x2 = mm_shared(("o", j), o_pre, lambda: _w(rwkv_w_o, (j,)),
                               lambda acc, ev: (ev[0] + ev[1] * acc[0],), F32,
                               n=d, tm=tm, tn=tn, extras=[_ex_row(x2, tm, tn), mod.mm_extra(1, 2, tm, tn)],
                               name=f"rwkv_o_{tag}")
                wkv_states.append(s_new)
            x2 = ffn(x2, mod, layer, 1, 2, tag + "b")
        y = _adanorm(x2.reshape(xshape), final_norm_g, None, 0, F32, tt, f"final_norm_{'s' if is_sample else 'p'}")
        return (y.reshape(n_seq, t_len, d), jnp.stack(kv_rows), jnp.stack(kr_rows), jnp.stack(conv_states),
                jnp.stack(shift_states), jnp.stack(wkv_states))

    y_s, s_kv, s_kr, s_conv, s_shift, s_wkv = run_group(x_sample.reshape(nsb, d), True)
    y_p, p_kv, p_kr, p_conv, p_shift, p_wkv = run_group(x_prompt.reshape(mp, d), False)
    return (y_p, y_s, p_kv, p_kr, p_conv, p_shift, p_wkv, s_kv, s_kr, s_conv, s_shift, s_wkv)
```

```python
import functools

import jax
import jax.numpy as jnp
from jax import lax
from jax.experimental import pallas as pl
from jax.experimental.pallas import tpu as pltpu

F32 = jnp.float32
BF16 = jnp.bfloat16

RMS_EPS = 1e-6
GN_EPS = 64e-5
ROPE_THETA = 10000.0
FFN_RES = 0.5
WKV_CHUNK = 64
WKV_LANES = 1024
ATTN_TQ = 512
ATTN_HEADS = 4
V7X_VMEM_LIMIT_BYTES = 56 * 1024 * 1024
HI = lax.Precision.HIGHEST


def _params(sem):
    return pltpu.CompilerParams(dimension_semantics=sem, vmem_limit_bytes=V7X_VMEM_LIMIT_BYTES)


def _dot(a, b, precision=None):
    return jnp.dot(a, b, preferred_element_type=F32, precision=precision)


def _dot_nt(a, b, precision=None):
    return lax.dot_general(a, b, (((1,), (1,)), ((), ())), preferred_element_type=F32, precision=precision)


def _dot_tn(a, b, precision=None):
    return lax.dot_general(a, b, (((0,), (0,)), ((), ())), preferred_element_type=F32, precision=precision)


def _sigmoid(x):
    return 1.0 / (1.0 + jnp.exp(-x))


def _silu(x):
    return x * _sigmoid(x)


def _w(arr, lead=(), roff=0, coff=0):
    return dict(arr=arr, lead=tuple(lead), roff=roff, coff=coff)


def _mm(xs, ws, accs, epi, out_dtypes, *, n, tm, tn, tk=None, kq=1, extras=(), x_pro=None, cast_ws=(),
        x_resident=False, ksplit=1, name):
    m, kdim = xs[0].shape
    tk = kdim if tk is None else tk
    assert m % tm == 0 and kdim % tk == 0, (name, m, tm, kdim, tk)
    assert not cast_ws or (m == tm and kq == 1)
    nchunk = kdim // tk
    nk = pl.cdiv(nchunk, kq)
    ragged = nk * kq != nchunk
    grid = (m // tm, pl.cdiv(n, tn), nk)
    nx, nw, ne, na, no = len(xs), len(ws), len(extras), len(accs), len(out_dtypes)

    def chunk(k, p):
        c = k * kq + p
        return jnp.minimum(c, nchunk - 1) if ragged else c

    in_specs = []
    for _ in xs:
        for p in range(kq):
            mode = dict(pipeline_mode=pl.Buffered(1)) if x_resident else {}
            in_specs.append(pl.BlockSpec((tm, tk), lambda i, j, k, p=p: (i, chunk(k, p)), **mode))
    for w in ws:
        nlead = len(w["lead"])
        assert w["arr"].ndim == nlead + 2
        for p in range(kq):
            def w_map(i, j, k, w=w, p=p):
                return w["lead"] + (w["roff"] + chunk(k, p), w["coff"] + j)

            in_specs.append(pl.BlockSpec((None,) * nlead + (tk, tn), w_map))
    for (_, bshape, imap) in extras:
        in_specs.append(pl.BlockSpec(bshape, imap))
    out_shapes = [jax.ShapeDtypeStruct((m, n), dt) for dt in out_dtypes]
    out_specs = [pl.BlockSpec((tm, tn), lambda i, j, k: (i, j)) for _ in out_dtypes]
    for wi in cast_ws:
        out_shapes.append(jax.ShapeDtypeStruct(ws[wi]["arr"].shape[-2:], BF16))
        out_specs.append(pl.BlockSpec((tk, tn), lambda i, j, k: (k, j)))
    nc = len(cast_ws)

    def kern(*refs):
        x_refs = refs[:nx * kq]
        w_refs = refs[nx * kq:(nx + nw) * kq]
        e_refs = refs[(nx + nw) * kq:(nx + nw) * kq + ne]
        o_refs = refs[(nx + nw) * kq + ne:(nx + nw) * kq + ne + no]
        c_refs = refs[(nx + nw) * kq + ne + no:(nx + nw) * kq + ne + no + nc]
        acc_refs = refs[(nx + nw) * kq + ne + no + nc:]
        k = pl.program_id(2)
        xv = []
        for xi in range(nx):
            pieces = []
            for p in range(kq):
                v = x_refs[xi * kq + p][...]
                if x_pro is not None:
                    v = x_pro(v.astype(F32))
                v = v.astype(BF16)
                if ragged and (nk - 1) * kq + p >= nchunk:
                    v = jnp.where(k * kq + p < nchunk, v, jnp.zeros_like(v))
                pieces.append(v)
            xv.append(pieces)
        wv = [[w_refs[wi * kq + p][...].astype(BF16) for p in range(kq)] for wi in range(nw)]
        for c_ref, wi in zip(c_refs, cast_ws):
            c_ref[...] = wv[wi][0]

        def accumulate(init):
            outs = []
            for ai, acc in enumerate(accs):
                tot = None if init is None else init[ai]
                for (xi, wi) in acc:
                    for p in range(kq):
                        for q in range(ksplit):
                            ks = slice(q * (tk // ksplit), (q + 1) * (tk // ksplit))
                            d = _dot(xv[xi][p][:, ks], wv[wi][p][ks, :])
                            tot = d if tot is None else tot + d
                outs.append(tot)
            return outs

        def finalize(vals):
            ev = []
            for r in e_refs:
                v = r[...]
                ev.append(v.reshape(v.shape[-2:]))
            outs = epi(vals, ev)
            for o_ref, o in zip(o_refs, outs):
                o_ref[...] = o.astype(o_ref.dtype).reshape(o_ref.shape)

        if nk == 1:
            finalize(accumulate(None))
        else:
            @pl.when(k == 0)
            def _():
                for a_ref, p in zip(acc_refs, accumulate(None)):
                    a_ref[...] = p

            if nk > 2:
                @pl.when(jnp.logical_and(k > 0, k < nk - 1))
                def _():
                    for a_ref, p in zip(acc_refs, accumulate([a_ref[...] for a_ref in acc_refs])):
                        a_ref[...] = p

            @pl.when(k == nk - 1)
            def _():
                finalize(accumulate([a_ref[...] for a_ref in acc_refs]))

    scratch = [pltpu.VMEM((tm, tn), F32) for _ in range(na)] if nk > 1 else []
    args = []
    for x in xs:
        args += [x] * kq
    for w in ws:
        args += [w["arr"]] * kq
    args += [e[0] for e in extras]
    outs = pl.pallas_call(
        kern, out_shape=out_shapes, grid=grid, in_specs=in_specs, out_specs=out_specs,
        scratch_shapes=scratch, name=name,
        compiler_params=_params(("parallel", "parallel", "arbitrary")),
    )(*args)
    return outs


def _ex_row(arr, tm, tn, coff=0):
    return (arr, (tm, tn), lambda i, j, k: (i, coff + j))


def _ex_col(arr, tn, coff=0):
    return (arr, (1, tn), lambda i, j, k: (0, coff + j))


class _Mod:
    def __init__(self, ada, d, per_row, row0, t_len):
        self.d, self.per_row, self.row0, self.t_len = d, per_row, row0, t_len
        rows = ada.shape[0]
        self.a2 = ada
        self.a3 = ada.reshape(rows, 1, ada.shape[1])
        self.a3r = ada.reshape(1, rows, ada.shape[1])

    def norm_operand(self, sub, kind, tt):
        slab = sub * 3 + kind
        if self.per_row:
            return self.a3r, (1, tt, self.d), (lambda b, t: (0, t, slab))
        row0 = self.row0
        return self.a3, (1, 1, self.d), (lambda b, t: (row0 + b, 0, slab))

    def mm_extra(self, sub, kind, tm, tn):
        cb = (sub * 3 + kind) * (self.d // tn)
        if self.per_row:
            return (self.a2, (tm, tn), lambda i, j, k: (i, cb + j))
        row0, t_len = self.row0, self.t_len
        return (self.a3, (1, 1, tn), lambda i, j, k: (row0 + (i * tm) // t_len, 0, cb + j))


def _adanorm(x3, g, mod, sub, out_dtype, tt, name):
    b, t, d = x3.shape
    use_mod = mod is not None

    def kern(*refs):
        if use_mod:
            x_ref, g_ref, sh_ref, sc_ref, o_ref = refs
        else:
            x_ref, g_ref, o_ref = refs
        xf = x_ref[0]
        y = xf * lax.rsqrt(jnp.mean(xf * xf, axis=-1, keepdims=True) + RMS_EPS) * g_ref[...]
        if use_mod:
            y = y * (1.0 + sc_ref[0]) + sh_ref[0]
        o_ref[0] = y.astype(o_ref.dtype)

    in_specs = [pl.BlockSpec((1, tt, d), lambda bi, ti: (bi, ti, 0)),
                pl.BlockSpec((1, d), lambda bi, ti: (0, 0))]
    args = [x3, g.reshape(1, d)]
    if use_mod:
        for kind in (0, 1):
            arr, bshape, imap = mod.norm_operand(sub, kind, tt)
            in_specs.append(pl.BlockSpec(bshape, imap))
            args.append(arr)
    return pl.pallas_call(
        kern, out_shape=jax.ShapeDtypeStruct((b, t, d), out_dtype), grid=(b, t // tt),
        in_specs=in_specs, out_specs=pl.BlockSpec((1, tt, d), lambda bi, ti: (bi, ti, 0)),
        name=name, compiler_params=_params(("parallel", "parallel")),
    )(*args)


def _rope_tables(pos, half):
    inv_freq = ROPE_THETA ** (-jnp.arange(half, dtype=F32) / half)
    ang = pos.astype(F32)[:, None] * inv_freq[None, :]
    cos, sin = jnp.cos(ang), jnp.sin(ang)
    return jnp.concatenate([cos, cos, -sin, sin], axis=-1)


def _rope_cols(w, half):
    return jnp.concatenate([w, w[..., half:], w[..., :half]], axis=-1)


def _kvprep(a, kr_raw, tab, q_norm_g, kv_norm_g, ql, kvl, rd, tm, name):
    m = a.shape[0]

    def kern(a_ref, kr_ref, tab_ref, qg_ref, kg_ref, qn_ref, ckv_ref, kro_ref, kcat_ref):
        av = a_ref[...]
        cq, ck = av[:, :ql], av[:, ql:]
        qn = cq * lax.rsqrt(jnp.mean(cq * cq, axis=-1, keepdims=True) + RMS_EPS) * qg_ref[...]
        qn_ref[...] = qn.astype(BF16)
        ckv = ck * lax.rsqrt(jnp.mean(ck * ck, axis=-1, keepdims=True) + RMS_EPS) * kg_ref[...]
        ckv_ref[...] = ckv
        r = kr_ref[...] * tab_ref[...]
        rope = r + pltpu.roll(r, rd, 1)
        kro_ref[...] = rope[:, :rd]
        lane = lax.broadcasted_iota(jnp.int32, rope.shape, 1)
        kcat_ref[...] = jnp.concatenate([ckv, jnp.where(lane < rd, rope, 0.0)], axis=1).astype(BF16)

    return pl.pallas_call(
        kern,
        out_shape=[jax.ShapeDtypeStruct((m, ql), BF16), jax.ShapeDtypeStruct((m, kvl), F32),
                   jax.ShapeDtypeStruct((m, rd), F32), jax.ShapeDtypeStruct((m, kvl + 2 * rd), BF16)],
        grid=(m // tm,),
        in_specs=[pl.BlockSpec((tm, ql + kvl), lambda i: (i, 0)),
                  pl.BlockSpec((tm, 2 * rd), lambda i: (i, 0)),
                  pl.BlockSpec((tm, 2 * rd), lambda i: (i, 0)),
                  pl.BlockSpec((1, ql), lambda i: (0, 0)),
                  pl.BlockSpec((1, kvl), lambda i: (0, 0))],
        out_specs=[pl.BlockSpec((tm, ql), lambda i: (i, 0)), pl.BlockSpec((tm, kvl), lambda i: (i, 0)),
                   pl.BlockSpec((tm, rd), lambda i: (i, 0)), pl.BlockSpec((tm, kvl + 2 * rd), lambda i: (i, 0))],
        name=name, compiler_params=_params(("parallel",)),
    )(a, kr_raw, tab, q_norm_g.reshape(1, ql), kv_norm_g.reshape(1, kvl))


def _qproj(qn, w_q, w_uk_t, tab, tm, name):
    m, ql = qn.shape
    nh, nd, kvl = w_uk_t.shape
    rd2 = w_q.shape[2] - nd

    def kern(qn_ref, wq_ref, wuk_ref, tab_ref, o_ref):
        qq = _dot(qn_ref[...], wq_ref[...].astype(BF16))
        q_lat = _dot(qq[:, :nd].astype(BF16), wuk_ref[...].astype(BF16))
        r = qq[:, nd:] * tab_ref[...]
        q_rope = r + pltpu.roll(r, rd2 // 2, 1)
        o_ref[...] = jnp.concatenate([q_lat, q_rope], axis=1).astype(BF16)

    return pl.pallas_call(
        kern, out_shape=jax.ShapeDtypeStruct((nh, m, kvl + rd2), BF16), grid=(nh, m // tm),
        in_specs=[pl.BlockSpec((tm, ql), lambda h, i: (i, 0)),
                  pl.BlockSpec((None, ql, nd + rd2), lambda h, i: (h, 0, 0)),
                  pl.BlockSpec((None, nd, kvl), lambda h, i: (h, 0, 0)),
                  pl.BlockSpec((tm, rd2), lambda h, i: (i, 0))],
        out_specs=pl.BlockSpec((None, tm, kvl + rd2), lambda h, i: (h, i, 0)),
        name=name, compiler_params=_params(("parallel", "parallel")),
    )(qn, w_q, w_uk_t, tab)


def _prompt_attention(qcat, kcat, w_uv, n_seq, t_len, scale, tq, hb, name):
    nh, m, e = qcat.shape
    _, kvl, vd = w_uv.shape
    nq = t_len // tq
    heads = range(hb)

    def kern(q_ref, k_ref, w_ref, o_ref, m_ref, l_ref, acc_ref, s_ref):
        qi = pl.program_id(2)
        m_ref[...] = jnp.full(m_ref.shape, -jnp.inf, F32)
        l_ref[...] = jnp.zeros(l_ref.shape, F32)
        acc_ref[...] = jnp.zeros(acc_ref.shape, F32)

        def scores(h, j):
            kb = k_ref[pl.ds(pl.multiple_of(j * tq, tq), tq), :]
            return _dot_nt(q_ref[h], kb) * scale

        def soft_pv(h, s, j):
            m_prev = m_ref[h]
            m_new = jnp.maximum(m_prev, jnp.max(s, axis=-1, keepdims=True))
            alpha = jnp.exp(m_prev - m_new)
            p = jnp.exp(s - m_new)
            l_ref[h] = l_ref[h] * alpha + jnp.sum(p, axis=-1, keepdims=True)
            m_ref[h] = m_new
            vb = k_ref[pl.ds(pl.multiple_of(j * tq, tq), tq), :kvl]
            acc_ref[h] = acc_ref[h] * alpha + _dot(p.astype(BF16), vb)

        for h in heads:
            s_ref[h] = scores(h, 0)

        def body(j, carry):
            ahead = min(2, hb)
            nxt = {h: scores(h, j + 1) for h in range(ahead)}
            for h in heads:
                s_cur = s_ref[h]
                s_ref[h] = nxt.pop(h)
                soft_pv(h, s_cur, j)
                if h + ahead < hb:
                    nxt[h + ahead] = scores(h + ahead, j + 1)
            return carry

        lax.fori_loop(0, qi, body, 0)
        row = lax.broadcasted_iota(jnp.int32, (tq, tq), 0)
        col = lax.broadcasted_iota(jnp.int32, (tq, tq), 1)
        for h in heads:
            soft_pv(h, jnp.where(col <= row, s_ref[h], -jnp.inf), qi)
        for h in heads:
            o_lat = (acc_ref[h] / l_ref[h]).astype(BF16)
            o_ref[:, h * vd:(h + 1) * vd] = _dot(o_lat, w_ref[h].astype(BF16)).astype(BF16)

    return pl.pallas_call(
        kern, out_shape=jax.ShapeDtypeStruct((m, nh * vd), BF16), grid=(n_seq, nh // hb, nq),
        in_specs=[pl.BlockSpec((hb, tq, e), lambda b, hg, qi: (hg, b * nq + qi, 0)),
                  pl.BlockSpec((t_len, e), lambda b, hg, qi: (b, 0)),
                  pl.BlockSpec((hb, kvl, vd), lambda b, hg, qi: (hg, 0, 0))],
        out_specs=pl.BlockSpec((tq, hb * vd), lambda b, hg, qi: (b * nq + qi, hg)),
        scratch_shapes=[pltpu.VMEM((hb, tq, 1), F32), pltpu.VMEM((hb, tq, 1), F32),
                        pltpu.VMEM((hb, tq, kvl), F32), pltpu.VMEM((hb, tq, tq), F32)],
        name=name, compiler_params=_params(("parallel", "parallel", "arbitrary")),
    )(qcat, kcat, w_uv)


def _decode_attention(q, cache_kv, cache_kr_t, layer, page_table, ckv_new, kr_new, scale, pp, name):
    bsz, nh, e = q.shape
    page, kvl = cache_kv.shape[2], cache_kv.shape[3]
    rd = cache_kr_t.shape[2]
    n_pages = page_table.shape[1]
    assert n_pages % pp == 0
    ns = n_pages // pp

    def kern(pt_ref, q_ref, *refs):
        kv_refs = refs[:pp]
        kr_refs = refs[pp:2 * pp]
        cn_ref, rn_ref, o_ref, m_ref, l_ref, acc_ref, kvb_ref, krb_ref = refs[2 * pp:]
        s_idx = pl.program_id(1)

        @pl.when(s_idx == 0)
        def _():
            m_ref[...] = jnp.full(m_ref.shape, -jnp.inf, F32)
            l_ref[...] = jnp.zeros(l_ref.shape, F32)
            acc_ref[...] = jnp.zeros(acc_ref.shape, F32)

        qv = q_ref[0]
        ql, qr = qv[:, :kvl], qv[:, kvl:kvl + rd]

        def online(s, pv_fn):
            m_prev = m_ref[...]
            m_new = jnp.maximum(m_prev, jnp.max(s, axis=-1, keepdims=True))
            alpha = jnp.exp(m_prev - m_new)
            p = jnp.exp(s - m_new)
            l_ref[...] = l_ref[...] * alpha + jnp.sum(p, axis=-1, keepdims=True)
            acc_ref[...] = acc_ref[...] * alpha + pv_fn(p)
            m_ref[...] = m_new

        for j in range(pp):
            kvb_ref[j * page:(j + 1) * page, :] = kv_refs[j][...].astype(BF16)
            krb_ref[:, j * page:(j + 1) * page] = kr_refs[j][...].astype(BF16)
        kv_all = kvb_ref[...]
        s = (_dot_nt(ql, kv_all) + _dot(qr, krb_ref[...])) * scale
        online(s, lambda p: _dot(p.astype(BF16), kv_all))

        @pl.when(s_idx == ns - 1)
        def _():
            cn = cn_ref[0].astype(BF16).astype(F32)
            rn = rn_ref[0].astype(BF16).astype(F32)
            s_new = (jnp.sum(ql.astype(F32) * cn, axis=-1, keepdims=True)
                     + jnp.sum(qr.astype(F32) * rn, axis=-1, keepdims=True)) * scale
            online(s_new, lambda p: p.astype(BF16).astype(F32) * cn)
            o_ref[0] = acc_ref[...] / l_ref[...]

    in_specs = [pl.BlockSpec((1, nh, e), lambda b, s, pt: (b, 0, 0))]
    for j in range(pp):
        in_specs.append(pl.BlockSpec((None, None, page, kvl),
                                     lambda b, s, pt, j=j: (layer, pt[b, s * pp + j], 0, 0)))
    for j in range(pp):
        in_specs.append(pl.BlockSpec((None, None, rd, page),
                                     lambda b, s, pt, j=j: (layer, pt[b, s * pp + j], 0, 0)))
    in_specs.append(pl.BlockSpec((1, 1, kvl), lambda b, s, pt: (b, 0, 0)))
    in_specs.append(pl.BlockSpec((1, 1, rd), lambda b, s, pt: (b, 0, 0)))
    grid_spec = pltpu.PrefetchScalarGridSpec(
        num_scalar_prefetch=1, grid=(bsz, ns), in_specs=in_specs,
        out_specs=pl.BlockSpec((1, nh, kvl), lambda b, s, pt: (b, 0, 0)),
        scratch_shapes=[pltpu.VMEM((nh, 1), F32), pltpu.VMEM((nh, 1), F32), pltpu.VMEM((nh, kvl), F32),
                        pltpu.VMEM((pp * page, kvl), BF16), pltpu.VMEM((rd, pp * page), BF16)])
    return pl.pallas_call(
        kern, out_shape=jax.ShapeDtypeStruct((bsz, nh, kvl), F32), grid_spec=grid_spec, name=name,
        compiler_params=_params(("parallel", "arbitrary")),
    )(page_table, q, *([cache_kv] * pp), *([cache_kr_t] * pp), ckv_new, kr_new)


def _oproj(o_lat, w_uv, tm, name):
    nh, m, kvl = o_lat.shape
    vd = w_uv.shape[2]

    def kern(o_ref, w_ref, y_ref):
        y_ref[...] = _dot(o_ref[...].astype(BF16), w_ref[...].astype(BF16)).astype(BF16)

    return pl.pallas_call(
        kern, out_shape=jax.ShapeDtypeStruct((m, nh * vd), BF16), grid=(nh, m // tm),
        in_specs=[pl.BlockSpec((None, tm, kvl), lambda h, i: (h, i, 0)),
                  pl.BlockSpec((None, kvl, vd), lambda h, i: (h, 0, 0))],
        out_specs=pl.BlockSpec((tm, vd), lambda h, i: (i, h)),
        name=name, compiler_params=_params(("parallel", "parallel")),
    )(o_lat, w_uv)


def _conv_prompt(bcu, conv_w, n_seq, t_len, cd, tc, name):
    m = bcu.shape[0]
    ncb = cd // tc
    cw = conv_w.shape[0]

    def kern(b_ref, c_ref, u_ref, w_ref, y_ref, st_ref):
        xg = c_ref[...] * u_ref[...]
        w = w_ref[...]
        row = lax.broadcasted_iota(jnp.int32, xg.shape, 0)
        y = xg * w[cw - 1:cw]
        for d in range(1, cw):
            sh = jnp.where(row >= d, pltpu.roll(xg, d, 0), 0.0)
            y = y + sh * w[cw - 1 - d:cw - d]
        y_ref[...] = (b_ref[...] * y).astype(BF16)
        st_ref[0] = xg[t_len - (cw - 1):, :]

    return pl.pallas_call(
        kern, out_shape=[jax.ShapeDtypeStruct((m, cd), BF16), jax.ShapeDtypeStruct((n_seq, cw - 1, cd), F32)],
        grid=(n_seq, ncb),
        in_specs=[pl.BlockSpec((t_len, tc), lambda b, j: (b, j)),
                  pl.BlockSpec((t_len, tc), lambda b, j: (b, ncb + j)),
                  pl.BlockSpec((t_len, tc), lambda b, j: (b, 2 * ncb + j)),
                  pl.BlockSpec((cw, tc), lambda b, j: (0, j))],
        out_specs=[pl.BlockSpec((t_len, tc), lambda b, j: (b, j)),
                   pl.BlockSpec((1, cw - 1, tc), lambda b, j: (b, 0, j))],
        name=name, compiler_params=_params(("parallel", "parallel")),
    )(bcu, bcu, bcu, conv_w)


def _conv_step(bcu, prev, conv_w, cd, tc, name):
    bsz = bcu.shape[0]
    ncb = cd // tc
    cw = conv_w.shape[0]

    def kern(*refs):
        b_ref, c_ref, u_ref, w_ref = refs[:4]
        p_refs = refs[4:4 + cw - 1]
        y_ref = refs[4 + cw - 1]
        st_refs = refs[5 + cw - 1:]
        xg = c_ref[...] * u_ref[...]
        w = w_ref[...]
        y = xg * w[cw - 1:cw]
        for d in range(cw - 1):
            y = y + p_refs[d][...] * w[d:d + 1]
        y_ref[...] = (b_ref[...] * y).astype(BF16)
        for d in range(cw - 2):
            st_refs[d][...] = p_refs[d + 1][...]
        st_refs[cw - 2][...] = xg

    in_specs = [pl.BlockSpec((bsz, tc), lambda j: (0, j)),
                pl.BlockSpec((bsz, tc), lambda j: (0, ncb + j)),
                pl.BlockSpec((bsz, tc), lambda j: (0, 2 * ncb + j)),
                pl.BlockSpec((cw, tc), lambda j: (0, j))]
    for d in range(cw - 1):
        in_specs.append(pl.BlockSpec((bsz, tc), lambda j, d=d: (0, d * ncb + j)))
    outs = pl.pallas_call(
        kern,
        out_shape=[jax.ShapeDtypeStruct((bsz, cd), BF16)] + [jax.ShapeDtypeStruct((bsz, cd), F32)] * (cw - 1),
        grid=(ncb,), in_specs=in_specs,
        out_specs=[pl.BlockSpec((bsz, tc), lambda j: (0, j))] * cw,
        name=name, compiler_params=_params(("parallel",)),
    )(bcu, bcu, bcu, conv_w, *([prev] * (cw - 1)))
    return outs[0], jnp.stack(outs[1:], axis=1)


def _token_mix(h3, prev3, mu, tc, name):
    b, t, d = h3.shape
    nmix = mu.shape[0]
    shift_here = prev3 is None

    def kern(*refs):
        if shift_here:
            h_ref, mu_ref = refs[:2]
            o_refs = refs[2:]
        else:
            h_ref, p_ref, mu_ref = refs[:3]
            o_refs = refs[3:]
        hh = h_ref[0]
        if shift_here:
            row = lax.broadcasted_iota(jnp.int32, hh.shape, 0)
            prev = jnp.where(row >= 1, pltpu.roll(hh, 1, 0), 0.0)
        else:
            prev = p_ref[0]
        xx = prev - hh
        mv = mu_ref[...]
        for i in range(nmix):
            o_refs[i][0] = (hh + xx * mv[i:i + 1]).astype(BF16)

    blk = pl.BlockSpec((1, t, tc), lambda bi, j: (bi, 0, j))
    in_specs = [blk] + ([] if shift_here else [blk]) + [pl.BlockSpec((nmix, tc), lambda bi, j: (0, j))]
    args = [h3] + ([] if shift_here else [prev3]) + [mu]
    return pl.pallas_call(
        kern, out_shape=[jax.ShapeDtypeStruct((b, t, d), BF16)] * nmix, grid=(b, d // tc),
        in_specs=in_specs, out_specs=[blk] * nmix,
        name=name, compiler_params=_params(("parallel", "parallel")),
    )(*args)


def _rowsum_mxu(xs, ones):
    his = [x.astype(BF16) for x in xs]
    los = [(x - hi.astype(F32)).astype(BF16) for x, hi in zip(xs, his)]
    return [_dot(hi, ones) + _dot(lo, ones) for hi, lo in zip(his, los)]


def _wkv_prompt(r, k, v, a, lw, g, k_k, k_a, r_k, ln_w, ln_b, n_seq, t_len, hs, name):
    m, d = r.shape
    c = WKV_CHUNK
    hl = min(WKV_LANES, d)
    hpb = hl // hs
    nh = d // hs
    nc = t_len // c

    def kern(r_ref, k_ref, v_ref, a_ref, lw_ref, g_ref, kk_ref, ka_ref, rk_ref, lnw_ref, lnb_ref,
             o_ref, s_ref):
        ci = pl.program_id(2)

        @pl.when(ci == 0)
        def _():
            s_ref[...] = jnp.zeros(s_ref.shape, F32)

        ri = lax.broadcasted_iota(jnp.int32, (c, c), 0)
        cj = lax.broadcasted_iota(jnp.int32, (c, c), 1)
        strict = cj < ri
        ri2 = lax.broadcasted_iota(jnp.int32, (c, 2 * c), 0)
        cj2 = lax.broadcasted_iota(jnp.int32, (c, 2 * c), 1)
        incl2 = jnp.where(cj2 >= c, cj2 - c, cj2) <= ri2
        tri = jnp.where(cj <= ri, 1.0, 0.0).astype(BF16)
        eye = jnp.where(cj == ri, 1.0, 0.0).astype(F32)
        heads = range(hpb)
        sls = [slice(hh * hs, (hh + 1) * hs) for hh in heads]

        lw_all = lw_ref[...]
        lw_hi = lw_all.astype(BF16)
        lw_lo = (lw_all - lw_hi.astype(F32)).astype(BF16)
        cum_all = _dot(tri, lw_hi) + _dot(tri, lw_lo)
        ones = jnp.ones((hs, hs), BF16)
        kk0 = [k_ref[:, sl] * kk_ref[:, sl] for sl in sls]
        ssq = _rowsum_mxu([x * x for x in kk0], ones)
        at, rt, bk_t, bk_h, vb, kmods, gtot = [], [], [], [], [], [], []
        for hh in heads:
            sl = sls[hh]
            rr, k0, aa, lwh = r_ref[:, sl], k_ref[:, sl], a_ref[:, sl], lw_all[:, sl]
            kk = kk0[hh] * lax.rsqrt(jnp.maximum(ssq[hh], 1e-24))
            kmod = k0 * (1.0 + (aa - 1.0) * ka_ref[:, sl])
            avec, bvec = -kk, kk * aa
            cum = cum_all[:, sl]
            tot = cum[c - 1:c, :]
            at.append((avec * jnp.exp(cum - lwh)).astype(BF16))
            rt.append((rr * jnp.exp(cum)).astype(BF16))
            e_neg = jnp.exp(-cum)
            e_tot = jnp.exp(tot - cum)
            bk_t.append(jnp.concatenate([bvec * e_neg, kmod * e_neg], axis=0).astype(BF16))
            bk_h.append(jnp.concatenate([bvec * e_tot, kmod * e_tot], axis=0).astype(BF16))
            vb.append(v_ref[:, sl].astype(BF16))
            kmods.append(kmod)
            gtot.append(jnp.exp(tot))
        a_ab = [jnp.where(strict, _dot_nt(at[hh], bk_t[hh][:c]), 0.0) for hh in heads]
        a_ak = [jnp.where(strict, _dot_nt(at[hh], bk_t[hh][c:]), 0.0).astype(BF16) for hh in heads]
        a_r = [jnp.where(incl2, _dot_nt(rt[hh], bk_t[hh]), 0.0).astype(BF16) for hh in heads]
        s0 = [s_ref[0, hh] for hh in heads]
        ws = [_dot_nt(jnp.concatenate([at[hh], rt[hh]], axis=0), s0[hh].astype(BF16)) for hh in heads]
        akv = [_dot(a_ak[hh], vb[hh]) for hh in heads]
        tinv = [eye + a_ab[hh] for hh in heads]
        lp = [a_ab[hh].astype(BF16) for hh in heads]
        n_done = 2
        while n_done < c:
            lp = [_dot(lp[hh], lp[hh]).astype(BF16) for hh in heads]
            tinv = [tinv[hh] + _dot(tinv[hh].astype(BF16), lp[hh]) for hh in heads]
            n_done *= 2
        p = [_dot(tinv[hh].astype(BF16), (ws[hh][:c] + akv[hh]).astype(BF16)) for hh in heads]
        z = [jnp.concatenate([p[hh].astype(BF16), vb[hh]], axis=0) for hh in heads]
        y = [ws[hh][c:] + _dot(a_r[hh], z[hh]) for hh in heads]
        s1 = [s0[hh] * gtot[hh] + _dot_tn(z[hh], bk_h[hh]) for hh in heads]
        rs = _rowsum_mxu([jnp.concatenate([y[hh], y[hh] * y[hh], r_ref[:, sls[hh]] * kmods[hh] * rk_ref[:, sls[hh]]],
                                          axis=0) for hh in heads], ones)
        for hh in heads:
            sl = sls[hh]
            s_ref[0, hh] = s1[hh]
            mean = rs[hh][:c] * (1.0 / hs)
            var = jnp.maximum(rs[hh][c:2 * c] * (1.0 / hs) - mean * mean, 0.0)
            yn = (y[hh] - mean) * lax.rsqrt(var + GN_EPS) * lnw_ref[:, sl] + lnb_ref[:, sl]
            o_ref[:, sl] = ((yn + rs[hh][2 * c:] * v_ref[:, sl]) * g_ref[:, sl]).astype(BF16)

    blk = pl.BlockSpec((c, hl), lambda b, hg, ci: (b * nc + ci, hg))
    wblk = pl.BlockSpec((1, hl), lambda b, hg, ci: (0, hg))
    return pl.pallas_call(
        kern, out_shape=[jax.ShapeDtypeStruct((m, d), BF16), jax.ShapeDtypeStruct((n_seq, nh, hs, hs), F32)],
        grid=(n_seq, d // hl, nc),
        in_specs=[blk] * 6 + [wblk] * 5,
        out_specs=[blk, pl.BlockSpec((1, hpb, hs, hs), lambda b, hg, ci: (b, hg, 0, 0))],
        name=name, compiler_params=_params(("parallel", "parallel", "arbitrary")),
    )(r, k, v, a, lw, g, k_k, k_a, r_k, ln_w, ln_b)


def _wkv_step(r, k, a, lw, v, g, s0, k_k, k_a, r_k, ln_w, ln_b, hb, name):
    nh, hs, bsz = r.shape

    def kern(r_ref, k_ref, a_ref, lw_ref, v_ref, g_ref, s_ref, kk_ref, ka_ref, rk_ref, lnw_ref, lnb_ref,
             o_ref, so_ref):
        rr, k0, aa = r_ref[...], k_ref[...], a_ref[...]
        vv = v_ref[...]
        dec = jnp.exp(lw_ref[...])
        kk = k0 * kk_ref[...]
        kk = kk * lax.rsqrt(jnp.maximum(jnp.sum(kk * kk, axis=1, keepdims=True), 1e-24))
        kmod = k0 * (1.0 + (aa - 1.0) * ka_ref[...])
        avec, bvec = -kk, kk * aa
        s = s_ref[...]
        sa = jnp.sum(s * avec[:, None], axis=2, keepdims=True)
        s1 = s * dec[:, None] + sa * bvec[:, None] + vv * kmod[:, None]
        so_ref[...] = s1
        y = jnp.sum(s1 * rr[:, None], axis=2, keepdims=True)
        mean = jnp.mean(y, axis=1, keepdims=True)
        var = jnp.mean(jnp.square(y - mean), axis=1, keepdims=True)
        yn = (y - mean) * lax.rsqrt(var + GN_EPS) * lnw_ref[...] + lnb_ref[...]
        bonus = jnp.sum(rr * kmod * rk_ref[...], axis=1, keepdims=True)[:, None] * vv
        o_ref[...] = (yn + bonus) * g_ref[...]

    kblk = pl.BlockSpec((hb, hs, bsz), lambda i: (i, 0, 0))
    vblk = pl.BlockSpec((hb, hs, 1, bsz), lambda i: (i, 0, 0, 0))
    sblk = pl.BlockSpec((hb, hs, hs, bsz), lambda i: (i, 0, 0, 0))
    wkblk = pl.BlockSpec((hb, hs, 1), lambda i: (i, 0, 0))
    wvblk = pl.BlockSpec((hb, hs, 1, 1), lambda i: (i, 0, 0, 0))
    return pl.pallas_call(
        kern, out_shape=[jax.ShapeDtypeStruct((nh, hs, 1, bsz), F32), jax.ShapeDtypeStruct((nh, hs, hs, bsz), F32)],
        grid=(nh // hb,),
        in_specs=[kblk] * 4 + [vblk] * 2 + [sblk] + [wkblk] * 3 + [wvblk] * 2,
        out_specs=[vblk, sblk],
        name=name, compiler_params=_params(("parallel",)),
    )(r, k, a, lw, v, g, s0, k_k, k_a, r_k, ln_w, ln_b)


def _pick(n, candidates):
    for c in candidates:
        if n % c == 0:
            return c
    return n


def kernel(x_prompt, x_sample, cache_kv_latent, cache_k_rope, state_conv, state_rwkv_shift, state_rwkv_wkv, page_table, c_prompt, c_sample, w_ada, b_ada, norm_g, ffn_w1, ffn_w3, ffn_w2, mla_w_in, mla_q_norm, mla_kv_norm, mla_w_uq, mla_w_ukv, conv_w, mix_w_out, rwkv_mu, rwkv_w0, rwkv_w1, rwkv_w2, rwkv_a0, rwkv_a1, rwkv_a2, rwkv_g1, rwkv_g2, rwkv_k_k, rwkv_k_a, rwkv_r_k, rwkv_w_r, rwkv_w_k, rwkv_w_v, rwkv_w_o, rwkv_ln_w, rwkv_ln_b, final_norm_g):
    nb, seq, d = x_prompt.shape
    nsb, dec_seq, _ = x_sample.shape
    assert dec_seq == 1
    depth = w_ada.shape[0]
    dff = ffn_w1.shape[-1]
    ql = mla_q_norm.shape[1]
    kvl = mla_kv_norm.shape[1]
    rd = cache_k_rope.shape[-1]
    nh_mla = mla_w_ukv.shape[2]
    nd = mla_w_uq.shape[2] // nh_mla - rd
    vd = mla_w_ukv.shape[3] - nd
    cd = conv_w.shape[-1]
    cw = conv_w.shape[1]
    hs = state_rwkv_wkv.shape[-1]
    nh_rwkv = d // hs
    past_len = page_table.shape[1] * cache_kv_latent.shape[2]
    mla_scale = float((nd + rd) ** -0.5)
    mp = nb * seq

    n_c = nsb + nb
    n_c_pad = -(-n_c // 16) * 16
    c_all = jnp.concatenate([c_sample, c_prompt, jnp.zeros((n_c_pad - n_c, d), F32)], axis=0)
    t_ada = _pick(9 * d, (512, 256, 128))
    adas = []
    for layer in range(depth):
        (ada,) = _mm([c_all], [_w(w_ada, (layer,))], [[(0, 0)]],
                     lambda acc, ev: (acc[0] + ev[0],), [F32],
                     n=9 * d, tm=n_c_pad, tn=t_ada, x_pro=_silu,
                     extras=[_ex_col(b_ada[layer].reshape(1, 9 * d), t_ada)], name=f"ada_l{layer}")
        adas.append(ada)

    ffn_bf16 = {}
    w_bf16 = {}

    def run_group(x2, is_sample):
        m = x2.shape[0]
        if is_sample:
            n_seq, t_len = m, 1
            xshape = (1, m, d)
            tm = m
            tt = m
            pos = jnp.full((m,), past_len, jnp.int32)
        else:
            n_seq, t_len = nb, seq
            xshape = (nb, seq, d)
            tm = _pick(seq, (1024, 512, 256, 128))
            tt = _pick(seq, (256, 128))
            pos = jnp.tile(jnp.arange(seq, dtype=jnp.int32), nb)
        tn = _pick(d, (512, 256, 128))
        tab = _rope_tables(pos, rd // 2)
        kv_rows, kr_rows, conv_states, shift_states, wkv_states = [], [], [], [], []

        def mm_shared(key, x, make_w, epi, out_dtype, **kw):
            if is_sample:
                o, wb = _mm([x], [make_w()], [[(0, 0)]], epi, [out_dtype], cast_ws=(0,), **kw)
                w_bf16[key] = wb
            else:
                (o,) = _mm([x], [_w(w_bf16[key])], [[(0, 0)]], epi, [out_dtype], **kw)
            return o

        def ffn(x2, mod, layer, idx, sub, tag):
            h = _adanorm(x2.reshape(xshape), norm_g[layer, sub], mod, sub, BF16, tt, f"norm_{tag}").reshape(m, d)
            tnf = _pick(dff, (256, 128))
            up_epi = lambda acc, ev: (_silu(acc[0]) * acc[1],)
            down_epi = lambda acc, ev: (ev[0] + FFN_RES * ev[1] * acc[0],)
            if is_sample:
                gact, w1b, w3b = _mm([h], [_w(ffn_w1, (layer, idx)), _w(ffn_w3, (layer, idx))],
                                     [[(0, 0)], [(0, 1)]], up_epi, [BF16],
                                     n=dff, tm=tm, tn=tnf, cast_ws=(0, 1), name=f"ffn_up_{tag}")
                tnd = _pick(d, (256, 128))
                xo, w2b = _mm([gact], [_w(ffn_w2, (layer, idx))], [[(0, 0)]], down_epi, [F32],
                              n=d, tm=tm, tn=tnd, cast_ws=(0,),
                              extras=[_ex_row(x2, tm, tnd), mod.mm_extra(sub, 2, tm, tnd)], name=f"ffn_down_{tag}")
                ffn_bf16[(layer, idx)] = (w1b, w3b, w2b)
            else:
                w1b, w3b, w2b = ffn_bf16[(layer, idx)]
                (gact,) = _mm([h], [_w(w1b), _w(w3b)], [[(0, 0)], [(0, 1)]], up_epi, [BF16],
                              n=dff, tm=tm, tn=tnf, name=f"ffn_up_{tag}")
                tnd = _pick(d, (256, 128))
                (xo,) = _mm([gact], [_w(w2b)], [[(0, 0)]], down_epi, [F32],
                            n=d, tm=tm, tn=tnd, x_resident=True, ksplit=2 if dff % 256 == 0 else 1,
                            extras=[_ex_row(x2, tm, tnd), mod.mm_extra(sub, 2, tm, tnd)], name=f"ffn_down_{tag}")
            return xo

        for layer in range(depth):
            tag = f"{'s' if is_sample else 'p'}{layer}"
            mod = _Mod(adas[layer], d, is_sample, nsb, t_len)
            x2 = ffn(x2, mod, layer, 0, 0, tag + "a")
            if layer % 2 == 0:
                i = layer // 2
                h = _adanorm(x2.reshape(xshape), norm_g[layer, 1], mod, 1, BF16, tt, f"norm_mix_{tag}").reshape(m, d)
                n_a = ql + kvl
                (a_proj,) = _mm([h], [_w(mla_w_in, (i,))], [[(0, 0)]], lambda acc, ev: (acc[0],), [F32],
                                n=n_a, tm=tm, tn=_pick(n_a, (512, 256, 128)), name=f"mla_in_a_{tag}")
                w_kr = _rope_cols(mla_w_in[i, :, n_a:n_a + rd], rd // 2)
                (kr_raw,) = _mm([h], [_w(w_kr)], [[(0, 0)]], lambda acc, ev: (acc[0],), [F32],
                                n=2 * rd, tm=tm, tn=2 * rd, name=f"mla_in_kr_{tag}")
                bcu = mm_shared(("bcu", i), h, lambda: _w(mla_w_in[i, :, n_a + rd:]), lambda acc, ev: (acc[0],), F32,
                                n=3 * cd, tm=tm, tn=_pick(cd, (512, 256, 128)), name=f"mla_in_bcu_{tag}")
                qn, ckv, kr, kcat = _kvprep(a_proj, kr_raw, tab, mla_q_norm[i], mla_kv_norm[i], ql, kvl, rd,
                                            _pick(m, (256, 128)), f"kvprep_{tag}")
                w_uq = mla_w_uq[i].reshape(ql, nh_mla, nd + rd)
                w_uq_h = jnp.transpose(w_uq, (1, 0, 2))
                w_q = jnp.concatenate([w_uq_h[:, :, :nd], _rope_cols(w_uq_h[:, :, nd:], rd // 2)], axis=-1)
                w_uk_t = jnp.transpose(mla_w_ukv[i][:, :, :nd], (1, 2, 0))
                w_uv = jnp.transpose(mla_w_ukv[i][:, :, nd:], (1, 0, 2))
                qcat = _qproj(qn, w_q, w_uk_t, tab, _pick(m, (1024, 512, 256, 128)), f"qproj_{tag}")
                if is_sample:
                    o_lat = _decode_attention(jnp.transpose(qcat, (1, 0, 2)), cache_kv_latent,
                                              jnp.swapaxes(cache_k_rope, 2, 3), i,
                                              page_table, ckv.reshape(m, 1, kvl), kr.reshape(m, 1, rd),
                                              mla_scale, _pick(page_table.shape[1], (64, 32, 16, 8, 4, 2, 1)), f"attn_{tag}")
                    o_att = _oproj(jnp.transpose(o_lat, (1, 0, 2)), w_uv, _pick(m, (512, 256, 128)), f"oproj_{tag}")
                    prev = state_conv[i].reshape(m, (cw - 1) * cd)
                    conv_y, conv_new = _conv_step(bcu, prev, conv_w[i], cd, _pick(cd, (512, 256, 128)), f"conv_{tag}")
                else:
                    o_att = _prompt_attention(qcat, kcat, w_uv, n_seq, t_len, mla_scale,
                                              _pick(t_len, (ATTN_TQ, 128)), _pick(nh_mla, (ATTN_HEADS, 2, 1)),
                                              f"attn_{tag}")
                    conv_y, conv_new = _conv_prompt(bcu, conv_w[i], n_seq, t_len, cd, _pick(cd, (256, 128)),
                                                    f"conv_{tag}")
                assert nh_mla * vd == cd
                (x2,) = _mm([o_att, conv_y], [_w(mix_w_out, (i,)), _w(mix_w_out, (i,), roff=1)], [[(0, 0), (1, 1)]],
                            lambda acc, ev: (ev[0] + ev[1] * acc[0],), [F32],
                            n=d, tm=tm, tn=tn, extras=[_ex_row(x2, tm, tn), mod.mm_extra(1, 2, tm, tn)],
                            name=f"mix_out_{tag}")
                kv_rows.append(ckv.reshape(n_seq, t_len, kvl))
                kr_rows.append(kr.reshape(n_seq, t_len, rd))
                conv_states.append(conv_new)
            else:
                j = layer // 2
                h3 = _adanorm(x2.reshape(xshape), norm_g[layer, 1], mod, 1, F32, tt, f"norm_mix_{tag}")
                tcm = _pick(d, (256, 128))
                if is_sample:
                    mixes = _token_mix(h3, state_rwkv_shift[j].reshape(xshape), rwkv_mu[j], tcm, f"tmix_{tag}")
                    shift_states.append(h3.reshape(m, d))
                else:
                    mixes = _token_mix(h3, None, rwkv_mu[j], tcm, f"tmix_{tag}")
                    shift_states.append(h3[:, -1])
                xr, xw, xk, xv, xa, xg = (t.reshape(m, d) for t in mixes)

                def proj(x, w, tag2):
                    return mm_shared((tag2, j), x, lambda: _w(w, (j,)), lambda acc, ev: (acc[0],), F32,
                                     n=d, tm=tm, tn=tn, name=f"rwkv_{tag2}_{tag}")

                r, k, v = proj(xr, rwkv_w_r, "r"), proj(xk, rwkv_w_k, "k"), proj(xv, rwkv_w_v, "v")
                dl = rwkv_w1.shape[-1]
                (t1,) = _mm([xw], [_w(rwkv_w1, (j,))], [[(0, 0)]], lambda acc, ev: (jnp.tanh(acc[0]),), [BF16],
                            n=dl, tm=tm, tn=dl, name=f"rwkv_w1_{tag}")

                def decay_epi(acc, ev):
                    z = -(ev[0] + acc[0])
                    softplus = jnp.maximum(z, 0.0) + jnp.log(1.0 + jnp.exp(-jnp.abs(z)))
                    return (-jnp.exp(-softplus - 0.5),)

                (lw,) = _mm([t1], [_w(rwkv_w2, (j,))], [[(0, 0)]], decay_epi, [F32],
                            n=d, tm=tm, tn=tn, extras=[_ex_col(rwkv_w0[j].reshape(1, d), tn)], name=f"rwkv_w2_{tag}")
                al = rwkv_a1.shape[-1]
                (t2,) = _mm([xa], [_w(rwkv_a1, (j,))], [[(0, 0)]], lambda acc, ev: (acc[0],), [BF16],
                            n=al, tm=tm, tn=al, name=f"rwkv_a1_{tag}")
                (a,) = _mm([t2], [_w(rwkv_a2, (j,))], [[(0, 0)]], lambda acc, ev: (_sigmoid(ev[0] + acc[0]),), [F32],
                           n=d, tm=tm, tn=tn, extras=[_ex_col(rwkv_a0[j].reshape(1, d), tn)], name=f"rwkv_a2_{tag}")
                gl = rwkv_g1.shape[-1]
                glp = -(-gl // 128) * 128
                g1p = jnp.pad(rwkv_g1[j], ((0, 0), (0, glp - gl)))
                g2p = jnp.pad(rwkv_g2[j], ((0, glp - gl), (0, 0)))
                (t3,) = _mm([xg], [_w(g1p)], [[(0, 0)]], lambda acc, ev: (_sigmoid(acc[0]),), [BF16],
                            n=glp, tm=tm, tn=glp, name=f"rwkv_g1_{tag}")
                (g,) = _mm([t3], [_w(g2p)], [[(0, 0)]], lambda acc, ev: (acc[0],), [F32],
                           n=d, tm=tm, tn=tn, name=f"rwkv_g2_{tag}")
                if is_sample:
                    tk3 = lambda t: t.T.reshape(nh_rwkv, hs, m)
                    tv4 = lambda t: t.T.reshape(nh_rwkv, hs, 1, m)
                    wk3 = lambda t: t.reshape(nh_rwkv, hs, 1)
                    wv4 = lambda t: t.reshape(nh_rwkv, hs, 1, 1)
                    o4, s_t = _wkv_step(tk3(r), tk3(k), tk3(a), tk3(lw), tv4(v), tv4(g),
                                        jnp.transpose(state_rwkv_wkv[j], (1, 2, 3, 0)),
                                        wk3(rwkv_k_k[j]), wk3(rwkv_k_a[j]), wk3(rwkv_r_k[j]),
                                        wv4(rwkv_ln_w[j]), wv4(rwkv_ln_b[j]),
                                        _pick(nh_rwkv, (2, 1)), f"wkv_{tag}")
                    o_pre = o4.reshape(d, m).T
                    s_new = jnp.transpose(s_t, (3, 0, 1, 2))
                else:
                    w2d = lambda t: t.reshape(1, d)
                    o_pre, s_new = _wkv_prompt(r, k, v, a, lw, g, w2d(rwkv_k_k[j]), w2d(rwkv_k_a[j]),
                                               w2d(rwkv_r_k[j]), w2d(rwkv_ln_w[j]), w2d(rwkv_ln_b[j]),
                                               n_seq, t_len, hs, f"wkv_{tag}")
                x2 = mm_shared(("o", j), o_pre, lambda: _w(rwkv_w_o, (j,)),
                               lambda acc, ev: (ev[0] + ev[1] * acc[0],), F32,
                               n=d, tm=tm, tn=tn, extras=[_ex_row(x2, tm, tn), mod.mm_extra(1, 2, tm, tn)],
                               name=f"rwkv_o_{tag}")
                wkv_states.append(s_new)
            x2 = ffn(x2, mod, layer, 1, 2, tag + "b")
        y = _adanorm(x2.reshape(xshape), final_norm_g, None, 0, F32, tt, f"final_norm_{'s' if is_sample else 'p'}")
        return (y.reshape(n_seq, t_len, d), jnp.stack(kv_rows), jnp.stack(kr_rows), jnp.stack(conv_states),
                jnp.stack(shift_states), jnp.stack(wkv_states))

    y_s, s_kv, s_kr, s_conv, s_shift, s_wkv = run_group(x_sample.reshape(nsb, d), True)
    y_p, p_kv, p_kr, p_conv, p_shift, p_wkv = run_group(x_prompt.reshape(mp, d), False)
    return (y_p, y_s, p_kv, p_kr, p_conv, p_shift, p_wkv, s_kv, s_kr, s_conv, s_shift, s_wkv)
```
